```python
import math
import jax, jax.numpy as jnp
from jax import lax
import numpy as np

D_MODEL = 1024
BATCH = 8
SEQ = 2048
DEPTH = 1
DEC_BATCH = 128
DEC_SEQ = 4
PAST_LEN = 16384
PAGE_SIZE = 128

R_HEADS = 4
R_DK = 128
R_DV = 128
M_HEADS = 4
M_DK = 128
M_DV = 128
R_WIDTH = R_HEADS * R_DV
M_WIDTH = M_HEADS * M_DV
MIX_WIDTH = R_WIDTH + M_WIDTH
M_CONV = 4
FF_CONV = 3
D_FF = 2816
CHUNK = 128
ROPE_THETA = 10000.0
RMS_EPS = 1e-6
GN_EPS = 1e-5
IN_WIDTHS = (R_HEADS * R_DK, R_HEADS * R_DK, R_WIDTH, R_WIDTH,
             M_HEADS * M_DK, M_HEADS * M_DK, M_WIDTH, M_WIDTH, M_HEADS, M_HEADS)
IN_COLS = sum(IN_WIDTHS)
MCONV_CH = 2 * M_HEADS * M_DK

kernel_name = 'hymba_retnet_mlstm_convffn_adaln_step'


def _offsets(widths):
    out, acc = [], 0
    for w in widths[:-1]:
        acc += w
        out.append(acc)
    return out


def rms_norm(x, w):
    xf = x.astype(jnp.float32)
    y = xf * lax.rsqrt(jnp.mean(xf * xf, axis=-1, keepdims=True) + RMS_EPS)
    return y * w.astype(jnp.float32)


def head_norm(x, w, n_heads):
    B, L, W = x.shape
    xf = x.astype(jnp.float32).reshape(B, L, n_heads, W // n_heads)
    mu = jnp.mean(xf, axis=-1, keepdims=True)
    xc = xf - mu
    var = jnp.mean(xc * xc, axis=-1, keepdims=True)
    return (xc * lax.rsqrt(var + GN_EPS)).reshape(B, L, W) * w.astype(jnp.float32)


def rotary(x, pos):
    half = x.shape[-1] // 2
    inv = ROPE_THETA ** (-jnp.arange(half, dtype=jnp.float32) / half)
    ang = pos.astype(jnp.float32)[:, None] * inv[None, :]
    cos, sin = jnp.cos(ang), jnp.sin(ang)
    x1, x2 = x[..., :half], x[..., half:]
    return jnp.concatenate([x1 * cos - x2 * sin, x2 * cos + x1 * sin], axis=-1)


def causal_dwconv(x, buf, w, b):
    K = w.shape[0]
    xp = jnp.concatenate([buf.astype(jnp.float32), x.astype(jnp.float32)], axis=1)
    y = lax.conv_general_dilated(xp, w[:, None, :].astype(jnp.float32), window_strides=(1,),
                                 padding='VALID', dimension_numbers=('NWC', 'WIO', 'NWC'),
                                 feature_group_count=x.shape[-1])
    return y + b.astype(jnp.float32), xp[:, xp.shape[1] - (K - 1):]


def to_chunks(a, cc):
    L = a.shape[2]
    a = a.reshape(a.shape[:2] + (L // cc, cc) + a.shape[3:])
    return jnp.moveaxis(a, 2, 0)


def from_chunks(a):
    a = jnp.moveaxis(a, 0, 2)
    return a.reshape(a.shape[:2] + (a.shape[2] * a.shape[3],) + a.shape[4:])


def retention_chunkwise(q, k, v, S0):
    H, L = q.shape[1], q.shape[2]
    cc = math.gcd(L, CHUNK)
    lg = jnp.log1p(-jnp.exp2(-5.0 - jnp.arange(H, dtype=jnp.float32)))
    idx = jnp.arange(cc, dtype=jnp.float32)
    rel = idx[:, None] - idx[None, :]
    inner = jnp.where(rel >= 0, jnp.exp(jnp.maximum(rel, 0.0) * lg[:, None, None]), 0.0)
    qdec = jnp.exp((idx + 1.0) * lg[:, None])
    kdec = jnp.exp((cc - 1.0 - idx) * lg[:, None])
    cdec = jnp.exp(cc * lg)

    def step(S, inp):
        qc, kc, vc = inp
        s = jnp.einsum('bhtd,bhsd->bhts', qc, kc) * inner
        o = (jnp.einsum('bhts,bhse->bhte', s, vc)
             + jnp.einsum('bhtd,bhde->bhte', qc * qdec[..., None], S))
        S = cdec[:, None, None] * S + jnp.einsum('bhsd,bhse->bhde', kc * kdec[..., None], vc)
        return S, o

    S, o = lax.scan(step, S0, (to_chunks(q, cc), to_chunks(k, cc), to_chunks(v, cc)))
    return from_chunks(o), S


def mlstm_chunkwise(q, k, v, ig, lf, C0, n0, m0):
    L = q.shape[2]
    cc = math.gcd(L, CHUNK)
    causal = jnp.tril(jnp.ones((cc, cc), dtype=bool))

    def step(carry, inp):
        C, n, m = carry
        qc, kc, vc, ic, fc = inp
        b = jnp.cumsum(fc, axis=-1)
        logw = jnp.where(causal, b[..., :, None] - b[..., None, :] + ic[..., None, :], -jnp.inf)
        inter = b + m[..., None]
        mt = jnp.maximum(inter, jnp.max(logw, axis=-1))
        s = jnp.einsum('bhtd,bhsd->bhts', qc, kc) * jnp.exp(logw - mt[..., None])
        wi = jnp.exp(inter - mt)
        num = (wi[..., None] * jnp.einsum('bhtd,bhde->bhte', qc, C)
               + jnp.einsum('bhts,bhse->bhte', s, vc))
        den = wi * jnp.einsum('bhtd,bhd->bht', qc, n) + jnp.sum(s, axis=-1)
        h = num / jnp.maximum(jnp.abs(den), jnp.exp(-mt))[..., None]
        m_new = mt[..., -1]
        wk = jnp.exp(b[..., -1:] - b + ic - m_new[..., None])
        wc = jnp.exp(b[..., -1] + m - m_new)
        C = wc[..., None, None] * C + jnp.einsum('bhsd,bhse->bhde', kc * wk[..., None], vc)
        n = wc[..., None] * n + jnp.einsum('bhs,bhsd->bhd', wk, kc)
        return (C, n, m_new), h

    (C, n, m), h = lax.scan(step, (C0, n0, m0),
                           (to_chunks(q, cc), to_chunks(k, cc), to_chunks(v, cc),
                            to_chunks(ig, cc), to_chunks(lf, cc)))
    return from_chunks(h), C, n, m


def _heads(a, n_heads):
    B, L, W = a.shape
    return a.reshape(B, L, n_heads, W // n_heads).transpose(0, 2, 1, 3)


def _merge(a):
    B, H, L, d = a.shape
    return a.transpose(0, 2, 1, 3).reshape(B, L, H * d)


def _layer(x, c, pos, st, p):
    S0, C0, n0, m0, mbuf, fbuf = st
    x = x.astype(jnp.float32)
    mod = jax.nn.silu(c.astype(jnp.float32)) @ p['w_ada'] + p['b_ada']
    sh1, sc1, g1, sh2, sc2, g2 = jnp.split(mod[:, None, :], 6, axis=-1)

    h = rms_norm(x, p['norm1_w']) * (1.0 + sc1) + sh1
    proj = h @ p['w_in']
    rq, rk, rv, rg, mq, mk, mv, mo, mi, mf = jnp.split(proj, _offsets(IN_WIDTHS), axis=-1)

    q_r = rotary(_heads(rq, R_HEADS), pos)
    k_r = rotary(_heads(rk, R_HEADS), pos) * (R_DK ** -0.5)
    o_r, S_new = retention_chunkwise(q_r, k_r, _heads(rv, R_HEADS), S0.astype(jnp.float32))
    y_r = head_norm(_merge(o_r), p['ret_norm_w'], R_HEADS) * jax.nn.silu(rg)

    qk, mbuf_new = causal_dwconv(jnp.concatenate([mq, mk], axis=-1), mbuf,
                                 p['mconv_w'], p['mconv_b'])
    mq2, mk2 = jnp.split(jax.nn.silu(qk), 2, axis=-1)
    ig = (mi + p['b_igate']).transpose(0, 2, 1)
    lf = jax.nn.log_sigmoid(mf + p['b_fgate']).transpose(0, 2, 1)
    h_m, C_new, n_new, m_new = mlstm_chunkwise(
        _heads(mq2, M_HEADS) * (M_DK ** -0.5), _heads(mk2, M_HEADS), _heads(mv, M_HEADS),
        ig, lf, C0.astype(jnp.float32), n0.astype(jnp.float32), m0.astype(jnp.float32))
    y_m = head_norm(_merge(h_m), p['mlstm_norm_w'], M_HEADS) * jax.nn.sigmoid(mo)

    x = x + g1 * (jnp.concatenate([y_r, y_m], axis=-1) @ p['w_out'])

    h2 = rms_norm(x, p['norm2_w']) * (1.0 + sc2) + sh2
    u, val = jnp.split(h2 @ p['w_up'], [D_FF], axis=-1)
    u, fbuf_new = causal_dwconv(u, fbuf, p['ffconv_w'], p['ffconv_b'])
    x = x + g2 * ((jax.nn.silu(u) * val) @ p['w_down'])
    return x, (S_new, C_new, n_new, m_new, mbuf_new, fbuf_new)


def setup_inputs(seed: int = 0) -> dict:
    key = jax.random.key(seed)
    ks = jax.random.split(key, 32)

    def nrm(k, shape, scale):
        return jax.random.normal(k, shape, jnp.float32) * scale

    return {
        'x_prompt': nrm(ks[0], (BATCH, SEQ, D_MODEL), 1.0),
        'x_sample': nrm(ks[1], (DEC_BATCH, DEC_SEQ, D_MODEL), 1.0),
        'c_prompt': nrm(ks[2], (BATCH, D_MODEL), 1.0),
        'c_sample': nrm(ks[3], (DEC_BATCH, D_MODEL), 1.0),
        'state_ret': nrm(ks[4], (DEPTH, DEC_BATCH, R_HEADS, R_DK, R_DV), 1.0),
        'state_mlstm_C': nrm(ks[5], (DEPTH, DEC_BATCH, M_HEADS, M_DK, M_DV), 1.0),
        'state_mlstm_n': nrm(ks[6], (DEPTH, DEC_BATCH, M_HEADS, M_DK), 0.5),
        'state_mlstm_m': nrm(ks[7], (DEPTH, DEC_BATCH, M_HEADS), 1.0),
        'state_mconv': nrm(ks[8], (DEPTH, DEC_BATCH, M_CONV - 1, MCONV_CH), 1.0),
        'state_ffconv': nrm(ks[9], (DEPTH, DEC_BATCH, FF_CONV - 1, D_FF), 1.0),
        'w_ada': nrm(ks[10], (DEPTH, D_MODEL, 6 * D_MODEL), D_MODEL ** -0.5),
        'b_ada': nrm(ks[11], (DEPTH, 6 * D_MODEL), 0.02),
        'norm1_w': 1.0 + nrm(ks[12], (DEPTH, D_MODEL), 0.02),
        'norm2_w': 1.0 + nrm(ks[13], (DEPTH, D_MODEL), 0.02),
        'w_in': nrm(ks[14], (DEPTH, D_MODEL, IN_COLS), D_MODEL ** -0.5),
        'b_igate': nrm(ks[15], (DEPTH, M_HEADS), 0.1),
        'b_fgate': jnp.linspace(3.0, 6.0, M_HEADS, dtype=jnp.float32)[None, :]
                   + nrm(ks[16], (DEPTH, M_HEADS), 0.1),
        'mconv_w': nrm(ks[17], (DEPTH, M_CONV, MCONV_CH), M_CONV ** -0.5),
        'mconv_b': nrm(ks[18], (DEPTH, MCONV_CH), 0.02),
        'ret_norm_w': 1.0 + nrm(ks[19], (DEPTH, R_WIDTH), 0.02),
        'mlstm_norm_w': 1.0 + nrm(ks[20], (DEPTH, M_WIDTH), 0.02),
        'w_out': nrm(ks[21], (DEPTH, MIX_WIDTH, D_MODEL), MIX_WIDTH ** -0.5),
        'w_up': nrm(ks[22], (DEPTH, D_MODEL, 2 * D_FF), D_MODEL ** -0.5),
        'ffconv_w': nrm(ks[23], (DEPTH, FF_CONV, D_FF), FF_CONV ** -0.5),
        'ffconv_b': nrm(ks[24], (DEPTH, D_FF), 0.02),
        'w_down': nrm(ks[25], (DEPTH, D_FF, D_MODEL), D_FF ** -0.5),
        'final_w': 1.0 + nrm(ks[26], (D_MODEL,), 0.02),
    }


def _stack(states, i, dtype):
    return jnp.stack([s[i] for s in states]).astype(dtype)


def reference(x_prompt, x_sample, c_prompt, c_sample, state_ret, state_mlstm_C, state_mlstm_n,
              state_mlstm_m, state_mconv, state_ffconv, w_ada, b_ada, norm1_w, norm2_w, w_in,
              b_igate, b_fgate, mconv_w, mconv_b, ret_norm_w, mlstm_norm_w, w_out, w_up,
              ffconv_w, ffconv_b, w_down, final_w):
    B, L = x_prompt.shape[0], x_prompt.shape[1]
    Bs, Ls = x_sample.shape[0], x_sample.shape[1]
    pos_p = jnp.arange(L, dtype=jnp.int32)
    pos_s = PAST_LEN + jnp.arange(Ls, dtype=jnp.int32)
    f32 = jnp.float32
    zero_state = (jnp.zeros((B, R_HEADS, R_DK, R_DV), f32),
                  jnp.zeros((B, M_HEADS, M_DK, M_DV), f32),
                  jnp.zeros((B, M_HEADS, M_DK), f32),
                  jnp.zeros((B, M_HEADS), f32),
                  jnp.zeros((B, M_CONV - 1, MCONV_CH), f32),
                  jnp.zeros((B, FF_CONV - 1, D_FF), f32))
    h_p, h_s = x_prompt, x_sample
    new_p, new_s = [], []
    for l in range(DEPTH):
        p = {'w_ada': w_ada[l], 'b_ada': b_ada[l], 'norm1_w': norm1_w[l], 'norm2_w': norm2_w[l],
             'w_in': w_in[l], 'b_igate': b_igate[l], 'b_fgate': b_fgate[l],
             'mconv_w': mconv_w[l], 'mconv_b': mconv_b[l], 'ret_norm_w': ret_norm_w[l],
             'mlstm_norm_w': mlstm_norm_w[l], 'w_out': w_out[l], 'w_up': w_up[l],
             'ffconv_w': ffconv_w[l], 'ffconv_b': ffconv_b[l], 'w_down': w_down[l]}
        h_p, st_p = _layer(h_p, c_prompt, pos_p, zero_state, p)
        st_in = (state_ret[l], state_mlstm_C[l], state_mlstm_n[l], state_mlstm_m[l],
                 state_mconv[l], state_ffconv[l])
        h_s, st_s = _layer(h_s, c_sample, pos_s, st_in, p)
        new_p.append(st_p)
        new_s.append(st_s)
    y_prompt = rms_norm(h_p, final_w).astype(x_prompt.dtype)
    y_sample = rms_norm(h_s, final_w).astype(x_sample.dtype)
    ret_p = _stack(new_p, 0, state_ret.dtype)
    mC_p = _stack(new_p, 1, state_mlstm_C.dtype)
    mn_p = _stack(new_p, 2, state_mlstm_n.dtype)
    mm_p = _stack(new_p, 3, state_mlstm_m.dtype)
    mconv_p = _stack(new_p, 4, state_mconv.dtype)
    ffconv_p = _stack(new_p, 5, state_ffconv.dtype)
    ret_s = _stack(new_s, 0, state_ret.dtype)
    mC_s = _stack(new_s, 1, state_mlstm_C.dtype)
    mn_s = _stack(new_s, 2, state_mlstm_n.dtype)
    mm_s = _stack(new_s, 3, state_mlstm_m.dtype)
    mconv_s = _stack(new_s, 4, state_mconv.dtype)
    ffconv_s = _stack(new_s, 5, state_ffconv.dtype)
    return (y_prompt, y_sample, ret_p, mC_p, mn_p, mm_p, mconv_p, ffconv_p,
            ret_s, mC_s, mn_s, mm_s, mconv_s, ffconv_s)
```

```python
import functools

import jax
import jax.numpy as jnp
from jax import lax
from jax.experimental import pallas as pl
from jax.experimental.pallas import tpu as pltpu

F32 = jnp.float32
BF16 = jnp.bfloat16

D_MODEL = 1024
N_HEADS = 4
HEAD = 128
GROUP_W = N_HEADS * HEAD
D_FF = 2816
M_CONV = 4
FF_CONV = 3
CHUNK = 128
PAST_LEN = 16384
ROPE_THETA = 10000.0
RMS_EPS = 1e-6
GN_EPS = 1e-5
NEG = -1e30

C_RQ, C_RK, C_RV, C_RG = 0, 512, 1024, 1536
C_MQ, C_MK, C_MV, C_MO = 2048, 2560, 3072, 3584
C_GATE = 4096
IN_COLS = 4104
IN_PAD = 4224
HIST = 8

V7X_VMEM_BYTES = 64 * 1024 * 1024
PROMPT_TILE = 256
SAMPLE_GROUP = 32
S_OUT_ROWS = 256


def _dot(a, b):
    return jnp.dot(a, b, preferred_element_type=F32)


def _dot_nt(a, b):
    return lax.dot_general(a, b, (((1,), (1,)), ((), ())), preferred_element_type=F32)


def _dot_tn(a, b):
    return lax.dot_general(a, b, (((0,), (0,)), ((), ())), preferred_element_type=F32)


def _bf(x):
    return x.astype(BF16)


def _silu(x):
    return x * (1.0 / (1.0 + jnp.exp(-x)))


def _sigmoid(x):
    return 1.0 / (1.0 + jnp.exp(-x))


def _log_sigmoid(x):
    return jnp.minimum(x, 0.0) - jnp.log(1.0 + jnp.exp(-jnp.abs(x)))


def _rms(x):
    return x * lax.rsqrt(jnp.mean(x * x, axis=-1, keepdims=True) + RMS_EPS)


def _head_norm(x, w_row):
    mu = jnp.mean(x, axis=-1, keepdims=True)
    xc = x - mu
    var = jnp.mean(xc * xc, axis=-1, keepdims=True)
    return xc * lax.rsqrt(var + GN_EPS) * w_row


def _rope(x, cos_t, sin_t):
    return x * cos_t + pltpu.roll(x, HEAD // 2, 1) * sin_t


def _pick_lane(x, idx):
    lane = lax.broadcasted_iota(jnp.int32, x.shape, 1)
    return jnp.sum(jnp.where(lane == idx, x, 0.0), axis=1, keepdims=True)


def _pick_row(x, idx):
    row = lax.broadcasted_iota(jnp.int32, x.shape, 0)
    return jnp.sum(jnp.where(row == idx, x, 0.0), axis=0, keepdims=True)


def _split3(x):
    hi = _bf(x)
    r = x - hi.astype(F32)
    mid = _bf(r)
    lo = _bf(r - mid.astype(F32))
    return hi, mid, lo


def _roll_rows(x, shift):
    shift = shift % x.shape[0]
    return x if shift == 0 else pltpu.roll(x, shift, 0)


class _ChunkOps:
    def __init__(self, rows, seq):
        self.rows, self.seq = rows, seq
        r = lax.broadcasted_iota(jnp.int32, (rows, rows), 0)
        c = lax.broadcasted_iota(jnp.int32, (rows, rows), 1)
        if seq == rows:
            self.causal = c <= r
            self.tril = jnp.where(self.causal, 1.0, 0.0).astype(BF16)
        else:
            shift = seq.bit_length() - 1
            same = (r >> shift) == (c >> shift)
            self.causal = same & (c <= r)
            self.t = lax.broadcasted_iota(jnp.int32, (rows, HEAD), 0) & (seq - 1)

    def cumsum(self, x):
        if self.seq == self.rows:
            hi, mid, lo = _split3(x)
            return _dot(self.tril, hi) + _dot(self.tril, mid) + _dot(self.tril, lo)
        out = x
        for d in range(1, self.seq):
            out = out + jnp.where(self.t >= d, _roll_rows(x, d), 0.0)
        return out

    def last(self, x):
        if self.seq == self.rows:
            return jnp.broadcast_to(x[self.rows - 1:self.rows, :], x.shape)
        y = jnp.where(self.t == self.seq - 1, x, 0.0)
        out = y
        for d in range(1, self.seq):
            out = out + _roll_rows(y, -d)
        return out

    def total(self, x):
        if self.seq == self.rows:
            return jnp.broadcast_to(jnp.sum(x, axis=0, keepdims=True), x.shape)
        assert self.seq == 4
        odd = (self.t & 1) == 1
        p = x + jnp.where(odd, _roll_rows(x, 1), _roll_rows(x, -1))
        return p + jnp.where(self.t >= 2, _roll_rows(p, 2), _roll_rows(p, -2))


def _gate_block(gates, gbias, ops):
    z = gates + gbias
    lane = lax.broadcasted_iota(jnp.int32, z.shape, 1)
    act = jnp.where(lane < N_HEADS, z, _log_sigmoid(z))
    csum = ops.cumsum(act)
    return act, csum, act.T, csum.T


def _ret_block(q, k, v, inner, qdec, kdec, state_read):
    s = _dot_nt(_bf(q), _bf(k)) * inner
    o = _dot(_bf(s), _bf(v)) + state_read(q * qdec)
    return o, k * kdec


def _mlstm_block(q, k, v, ig_col, ig_row, b_col, b_row, m_col, n_rows, ops, state_read):
    rows = q.shape[0]
    logw = jnp.where(ops.causal, b_col - b_row + ig_row, NEG)
    inter = b_col + m_col
    mt = jnp.maximum(inter, jnp.max(logw, axis=-1, keepdims=True))
    s = _dot_nt(_bf(q), _bf(k)) * jnp.exp(logw - mt)
    wi = jnp.exp(inter - mt)
    num = wi * state_read(q) + _dot(_bf(s), _bf(v))
    den = wi * jnp.sum(q * n_rows, axis=-1, keepdims=True) + jnp.sum(s, axis=-1, keepdims=True)
    h = num / jnp.maximum(jnp.abs(den), jnp.exp(-mt))
    b_last = ops.last(jnp.broadcast_to(b_col, (rows, HEAD)))
    m_new = ops.last(jnp.broadcast_to(mt, (rows, HEAD)))
    wk = jnp.exp(b_last - b_col + ig_col - m_new)
    wc = jnp.exp(b_last + m_col - m_new)
    kw = k * wk
    n_new = wc * n_rows + ops.total(kw)
    return h, kw, wc, m_new, n_new


def _ada_kernel(c_ref, w_ref, b_ref, o_ref):
    c = c_ref[...]
    o_ref[...] = _dot(_bf(_silu(c)), _bf(w_ref[...])) + b_ref[...]


def _ada_call(c_all, w_ada, b_ada):
    rows = c_all.shape[0]
    n_blk = w_ada.shape[1] // D_MODEL
    return pl.pallas_call(
        _ada_kernel,
        grid=(n_blk,),
        in_specs=[pl.BlockSpec((rows, D_MODEL), lambda n: (0, 0)),
                  pl.BlockSpec((D_MODEL, D_MODEL), lambda n: (0, n)),
                  pl.BlockSpec((1, D_MODEL), lambda n: (0, n))],
        out_specs=pl.BlockSpec((rows, D_MODEL), lambda n: (0, n)),
        out_shape=jax.ShapeDtypeStruct((rows, w_ada.shape[1]), F32),
        name="ada",
    )(c_all, w_ada, b_ada)


def _ffn(x1, a2, sh2, g2, nf, wupu_ref, wupv_ref, wdn_ref, fcw_ref, fcb_ref, conv_fn):
    h2 = _bf(_rms(x1) * a2 + sh2)
    u = _dot(h2, wupu_ref[...])
    val = _dot(h2, wupv_ref[...])
    uc = conv_fn(u)
    act = _bf(_silu(uc) * val)
    x2 = x1 + g2 * _dot(act, wdn_ref[...])
    return _rms(x2) * nf, u


def _prompt_kernel(x_ref, mod_ref, n1_ref, n2_ref, nf_ref, win_ref, gb_ref, mcw_ref, mcb_ref,
                   rnw_ref, mnw_ref, wout_ref, wupu_ref, wupv_ref, fcw_ref, fcb_ref, wdn_ref,
                   cos_ref, sin_ref, inner_ref, qdec_ref, kdec_ref, cdec_ref,
                   y_ref, ret_ref, mc_ref, mn_ref, mm_ref, mconv_ref, ffconv_ref,
                   proj_sc, qk_ext, u_ext, ymix_sc, x1_sc, s_sc, c_sc, n_sc, m_sc, *, tile):
    j = pl.program_id(1)
    last_j = pl.num_programs(1) - 1

    @pl.when(j == 0)
    def _():
        s_sc[...] = jnp.zeros_like(s_sc)
        c_sc[...] = jnp.zeros_like(c_sc)
        n_sc[...] = jnp.zeros_like(n_sc)
        m_sc[...] = jnp.zeros_like(m_sc)
        qk_ext[0:HIST, :] = jnp.zeros((HIST, 2 * GROUP_W), F32)
        u_ext[0:HIST, :] = jnp.zeros((HIST, D_FF), F32)

    sh1, sc1, g1 = mod_ref[0, 0:1, :], mod_ref[0, 1:2, :], mod_ref[0, 2:3, :]
    sh2, sc2, g2 = mod_ref[0, 3:4, :], mod_ref[0, 4:5, :], mod_ref[0, 5:6, :]

    h = _bf(_rms(x_ref[0]) * (n1_ref[...] * (1.0 + sc1)) + sh1)
    proj_sc[...] = _dot(h, win_ref[:, 0:C_GATE])
    gates = _dot(h, win_ref[:, C_GATE:IN_PAD])

    qk_ext[HIST:HIST + tile, :] = proj_sc[:, C_MQ:C_MV]
    conv = mcb_ref[...] + mcw_ref[M_CONV - 1:M_CONV, :] * qk_ext[HIST:HIST + tile, :]
    for t in range(M_CONV - 1):
        off = HIST - (M_CONV - 1) + t
        conv = conv + mcw_ref[t:t + 1, :] * qk_ext[off:off + tile, :]
    qkc = _silu(conv)

    ops = _ChunkOps(CHUNK, CHUNK)
    for c in range(tile // CHUNK):
        r0 = c * CHUNK
        rows = slice(r0, r0 + CHUNK)
        cos_t, sin_t = cos_ref[rows, :], sin_ref[rows, :]
        act, csum, act_t, csum_t = _gate_block(gates[rows, :], gb_ref[...], ops)
        for hd in range(N_HEADS):
            cols = slice(hd * HEAD, (hd + 1) * HEAD)
            q = _rope(proj_sc[rows, C_RQ + hd * HEAD:C_RQ + (hd + 1) * HEAD], cos_t, sin_t)
            k = _rope(proj_sc[rows, C_RK + hd * HEAD:C_RK + (hd + 1) * HEAD], cos_t, sin_t) * (HEAD ** -0.5)
            v = proj_sc[rows, C_RV + hd * HEAD:C_RV + (hd + 1) * HEAD]
            o, kd = _ret_block(q, k, v, inner_ref[hd], qdec_ref[hd], kdec_ref[hd],
                               lambda qd, hd=hd: _dot(_bf(qd), _bf(s_sc[hd])))
            s_sc[hd] = cdec_ref[hd] * s_sc[hd] + _dot_tn(_bf(kd), _bf(v))
            gate = _silu(proj_sc[rows, C_RG + hd * HEAD:C_RG + (hd + 1) * HEAD])
            ymix_sc[rows, cols] = _bf(_head_norm(o, rnw_ref[:, cols]) * gate)
            mq = qkc[rows, hd * HEAD:(hd + 1) * HEAD] * (HEAD ** -0.5)
            mk = qkc[rows, GROUP_W + hd * HEAD:GROUP_W + (hd + 1) * HEAD]
            mv = proj_sc[rows, C_MV + hd * HEAD:C_MV + (hd + 1) * HEAD]
            hm, kw, wc, m_new, n_new = _mlstm_block(
                mq, mk, mv,
                _pick_lane(act, hd), _pick_row(act_t, hd),
                _pick_lane(csum, N_HEADS + hd), _pick_row(csum_t, N_HEADS + hd),
                m_sc[hd][:, 0:1], n_sc[hd], ops,
                lambda qq, hd=hd: _dot(_bf(qq), _bf(c_sc[hd])))
            c_sc[hd] = wc * c_sc[hd] + _dot_tn(_bf(kw), _bf(mv))
            n_sc[hd] = n_new
            m_sc[hd] = m_new
            ogate = _sigmoid(proj_sc[rows, C_MO + hd * HEAD:C_MO + (hd + 1) * HEAD])
            ymix_sc[rows, GROUP_W + hd * HEAD:GROUP_W + (hd + 1) * HEAD] = _bf(
                _head_norm(hm, mnw_ref[:, cols]) * ogate)

    qk_ext[0:HIST, :] = qk_ext[tile:tile + HIST, :]

    x1_sc[...] = x_ref[0] + g1 * _dot(ymix_sc[...], wout_ref[...])

    def conv_fn(u):
        u_ext[HIST:HIST + tile, :] = u
        uc = fcb_ref[...] + fcw_ref[FF_CONV - 1:FF_CONV, :] * u
        for t in range(FF_CONV - 1):
            off = HIST - (FF_CONV - 1) + t
            uc = uc + fcw_ref[t:t + 1, :] * u_ext[off:off + tile, :]
        return uc

    y, _ = _ffn(x1_sc[...], n2_ref[...] * (1.0 + sc2), sh2, g2, nf_ref[...],
                wupu_ref, wupv_ref, wdn_ref, fcw_ref, fcb_ref, conv_fn)
    y_ref[0] = y
    u_ext[0:HIST, :] = u_ext[tile:tile + HIST, :]

    @pl.when(j == last_j)
    def _():
        ret_ref[0] = s_sc[...]
        mc_ref[0] = c_sc[...]
        for hd in range(N_HEADS):
            mn_ref[0, hd:hd + 1, :] = n_sc[hd][0:1, :]
            mm_ref[0, hd:hd + 1, :] = m_sc[hd][0:1, :]
        mconv_ref[0] = qk_ext[0:HIST, :]
        ffconv_ref[0] = u_ext[0:HIST, :]


def _const_spec(shape):
    nd = len(shape)
    return pl.BlockSpec(shape, lambda *_: (0,) * nd, pipeline_mode=pl.Buffered(1))


def _prompt_call(x, mod, n1, n2, nf, win, gbias, mcw, mcb, rnw, mnw, wout, wupu, wupv, fcw, fcb, wdn,
                 cos_t, sin_t, consts):
    nb, seq, _ = x.shape
    tile = PROMPT_TILE
    nt = seq // tile
    inner, qdec, kdec, cdec = consts
    state = (N_HEADS, HEAD, HEAD)
    in_specs = [
        pl.BlockSpec((1, tile, D_MODEL), lambda b, j: (b, j, 0)),
        pl.BlockSpec((1, 6, D_MODEL), lambda b, j: (b, 0, 0)),
        _const_spec((1, D_MODEL)), _const_spec((1, D_MODEL)), _const_spec((1, D_MODEL)),
        _const_spec((D_MODEL, IN_PAD)), _const_spec((1, HEAD)),
        _const_spec((M_CONV, 2 * GROUP_W)), _const_spec((1, 2 * GROUP_W)),
        _const_spec((1, GROUP_W)), _const_spec((1, GROUP_W)),
        _const_spec((2 * GROUP_W, D_MODEL)),
        _const_spec((D_MODEL, D_FF)), _const_spec((D_MODEL, D_FF)),
        _const_spec((FF_CONV, D_FF)), _const_spec((1, D_FF)),
        _const_spec((D_FF, D_MODEL)),
        pl.BlockSpec((tile, HEAD), lambda b, j: (j, 0)),
        pl.BlockSpec((tile, HEAD), lambda b, j: (j, 0)),
        _const_spec(state), _const_spec(state), _const_spec(state), _const_spec(state),
    ]
    out_specs = [
        pl.BlockSpec((1, tile, D_MODEL), lambda b, j: (b, j, 0)),
        pl.BlockSpec((1,) + state, lambda b, j: (b, 0, 0, 0)),
        pl.BlockSpec((1,) + state, lambda b, j: (b, 0, 0, 0)),
        pl.BlockSpec((1, N_HEADS, HEAD), lambda b, j: (b, 0, 0)),
        pl.BlockSpec((1, N_HEADS, HEAD), lambda b, j: (b, 0, 0)),
        pl.BlockSpec((1, HIST, 2 * GROUP_W), lambda b, j: (b, 0, 0)),
        pl.BlockSpec((1, HIST, D_FF), lambda b, j: (b, 0, 0)),
    ]
    out_shape = [
        jax.ShapeDtypeStruct((nb, seq, D_MODEL), F32),
        jax.ShapeDtypeStruct((nb,) + state, F32),
        jax.ShapeDtypeStruct((nb,) + state, F32),
        jax.ShapeDtypeStruct((nb, N_HEADS, HEAD), F32),
        jax.ShapeDtypeStruct((nb, N_HEADS, HEAD), F32),
        jax.ShapeDtypeStruct((nb, HIST, 2 * GROUP_W), F32),
        jax.ShapeDtypeStruct((nb, HIST, D_FF), F32),
    ]
    scratch = [
        pltpu.VMEM((tile, C_GATE), F32),
        pltpu.VMEM((tile + HIST, 2 * GROUP_W), F32),
        pltpu.VMEM((tile + HIST, D_FF), F32),
        pltpu.VMEM((tile, 2 * GROUP_W), BF16),
        pltpu.VMEM((tile, D_MODEL), F32),
        pltpu.VMEM(state, F32), pltpu.VMEM(state, F32), pltpu.VMEM(state, F32), pltpu.VMEM(state, F32),
    ]
    return pl.pallas_call(
        functools.partial(_prompt_kernel, tile=tile),
        grid=(nb, nt),
        in_specs=in_specs, out_specs=out_specs, out_shape=out_shape, scratch_shapes=scratch,
        compiler_params=pltpu.CompilerParams(
            dimension_semantics=("arbitrary", "arbitrary"),
            vmem_limit_bytes=V7X_VMEM_BYTES - 4 * 1024 * 1024),
        name="prompt_layer",
    )(x, mod, n1, n2, nf, win, gbias, mcw, mcb, rnw, mnw, wout, wupu, wupv, fcw, fcb, wdn,
      cos_t, sin_t, inner, qdec, kdec, cdec)


def _s_in_kernel(x_ref, sh1_ref, sc1_ref, n1_ref, win_ref, o_ref):
    h = _bf(_rms(x_ref[...]) * (n1_ref[...] * (1.0 + sc1_ref[...])) + sh1_ref[...])
    o_ref[...] = _dot(h, win_ref[...])


def _s_in_call(xs, mod_rows, n1, win):
    rows = xs.shape[0]
    nblk = 3
    wblk = IN_PAD // nblk
    return pl.pallas_call(
        _s_in_kernel,
        grid=(nblk,),
        in_specs=[pl.BlockSpec((rows, D_MODEL), lambda n: (0, 0)),
                  pl.BlockSpec((rows, D_MODEL), lambda n: (0, 0)),
                  pl.BlockSpec((rows, D_MODEL), lambda n: (0, 1)),
                  pl.BlockSpec((1, D_MODEL), lambda n: (0, 0)),
                  pl.BlockSpec((D_MODEL, wblk), lambda n: (0, n))],
        out_specs=pl.BlockSpec((rows, wblk), lambda n: (0, n)),
        out_shape=jax.ShapeDtypeStruct((rows, IN_PAD), F32),
        name="sample_in",
    )(xs, mod_rows, mod_rows, n1, win)


def _s_mix_kernel(rq_ref, rk_ref, rv_ref, rg_ref, mq_ref, mk_ref, mv_ref, mo_ref, gate_ref,
                  hq_ref, hk_ref, s0_ref, c0_ref, n0_ref, m0_ref, gb_ref,
                  mcwq_ref, mcwk_ref, mcbq_ref, mcbk_ref, rnw_ref, mnw_ref,
                  cos_ref, sin_ref, inner_ref, qdec_ref, kdec_ref, cdec_ref,
                  yr_ref, ym_ref, s1_ref, c1_ref, n1_ref, m1_ref, o_sc, *, seq):
    hd = pl.program_id(1)
    rows = rq_ref.shape[0]
    nbatch = rows // seq
    ops = _ChunkOps(rows, seq)
    row_t = lax.broadcasted_iota(jnp.int32, (rows, HEAD), 0) & (seq - 1)
    tile_half = lax.broadcasted_iota(jnp.int32, (2 * seq, HEAD), 0) >= seq
    lane_b = lax.broadcasted_iota(jnp.int32, (HEAD, rows), 1) >> (seq.bit_length() - 1)

    def state_read(state_ref):
        def read(qq):
            for i in range(nbatch // 2):
                qt = qq[2 * seq * i:2 * seq * (i + 1), :]
                lo = _dot(_bf(jnp.where(tile_half, 0.0, qt)), _bf(state_ref[2 * i, 0]))
                hi = _dot(_bf(jnp.where(tile_half, qt, 0.0)), _bf(state_ref[2 * i + 1, 0]))
                o_sc[2 * seq * i:2 * seq * (i + 1), :] = lo + hi
            return o_sc[...]
        return read

    def state_write(new_ref, old_ref, decay_rows, kx, v):
        kx_t = kx.T
        vb = _bf(v)
        for b in range(nbatch):
            upd = _dot(_bf(jnp.where(lane_b == b, kx_t, 0.0)), vb)
            dec = jnp.broadcast_to(decay_rows[seq * b:seq * b + 1, :], (HEAD, HEAD))
            new_ref[b, 0] = dec * old_ref[b, 0] + upd

    cos_t, sin_t = cos_ref[...], sin_ref[...]
    q = _rope(rq_ref[...], cos_t, sin_t)
    k = _rope(rk_ref[...], cos_t, sin_t) * (HEAD ** -0.5)
    v = rv_ref[...]
    o, kd = _ret_block(q, k, v, inner_ref[0], qdec_ref[0], kdec_ref[0], state_read(s0_ref))
    state_write(s1_ref, s0_ref, cdec_ref[0], kd, v)
    yr_ref[...] = _bf(_head_norm(o, rnw_ref[...]) * _silu(rg_ref[...]))

    def conv(x, hist, w_ref, b_ref):
        out = b_ref[...] + w_ref[M_CONV - 1:M_CONV, :] * x
        for d in range(1, M_CONV):
            prev = jnp.where(row_t >= d, _roll_rows(x, d), _roll_rows(hist, d - (M_CONV - 1)))
            out = out + w_ref[M_CONV - 1 - d:M_CONV - d, :] * prev
        return out

    mq = _silu(conv(mq_ref[...], hq_ref[...], mcwq_ref, mcbq_ref)) * (HEAD ** -0.5)
    mk = _silu(conv(mk_ref[...], hk_ref[...], mcwk_ref, mcbk_ref))
    mv = mv_ref[...]
    act, csum, act_t, csum_t = _gate_block(gate_ref[...], gb_ref[...], ops)
    m_col = m0_ref[0]
    hm, kw, wc, m_new, n_new = _mlstm_block(
        mq, mk, mv,
        _pick_lane(act, hd), _pick_row(act_t, hd),
        _pick_lane(csum, N_HEADS + hd), _pick_row(csum_t, N_HEADS + hd),
        m_col, n0_ref[...], ops, state_read(c0_ref))
    state_write(c1_ref, c0_ref, wc, kw, mv)
    n1_ref[...] = n_new
    m1_ref[...] = m_new
    ym_ref[...] = _bf(_head_norm(hm, mnw_ref[...]) * _sigmoid(mo_ref[...]))


def _s_mix_call(proj, hist, s0, c0, n_rows, m_rows, gbias, mcw, mcb, rnw, mnw, cos_t, sin_t, consts, seq):
    rows_all = proj.shape[0]
    blk = SAMPLE_GROUP * seq
    ng = rows_all // blk
    inner, qdec, kdec, cdec = consts

    def col(cb):
        return pl.BlockSpec((blk, HEAD), lambda g, h, cb=cb: (g, cb + h))

    def head_const(arr_rows):
        return pl.BlockSpec((arr_rows, HEAD), lambda g, h: (0, h))

    st_spec = pl.BlockSpec((SAMPLE_GROUP, 1, HEAD, HEAD), lambda g, h: (g, h, 0, 0))
    hconst = pl.BlockSpec((1, HEAD, HEAD), lambda g, h: (h, 0, 0))
    in_specs = [
        col(C_RQ // HEAD), col(C_RK // HEAD), col(C_RV // HEAD), col(C_RG // HEAD),
        col(C_MQ // HEAD), col(C_MK // HEAD), col(C_MV // HEAD), col(C_MO // HEAD),
        pl.BlockSpec((blk, HEAD), lambda g, h: (g, C_GATE // HEAD)),
        col(0), col(N_HEADS),
        st_spec, st_spec,
        pl.BlockSpec((blk, HEAD), lambda g, h: (g, h)),
        pl.BlockSpec((1, blk, 1), lambda g, h: (h, g, 0)),
        pl.BlockSpec((1, HEAD), lambda g, h: (0, 0)),
        head_const(M_CONV), pl.BlockSpec((M_CONV, HEAD), lambda g, h: (0, N_HEADS + h)),
        head_const(1), pl.BlockSpec((1, HEAD), lambda g, h: (0, N_HEADS + h)),
        head_const(1), head_const(1),
        pl.BlockSpec((blk, HEAD), lambda g, h: (0, 0)),
        pl.BlockSpec((blk, HEAD), lambda g, h: (0, 0)),
        hconst, hconst, hconst, hconst,
    ]
    out_specs = [
        pl.BlockSpec((blk, HEAD), lambda g, h: (g, h)),
        pl.BlockSpec((blk, HEAD), lambda g, h: (g, h)),
        st_spec, st_spec,
        pl.BlockSpec((blk, HEAD), lambda g, h: (g, h)),
        pl.BlockSpec((blk, HEAD), lambda g, h: (g, h)),
    ]
    out_shape = [
        jax.ShapeDtypeStruct((rows_all, GROUP_W), BF16),
        jax.ShapeDtypeStruct((rows_all, GROUP_W), BF16),
        jax.ShapeDtypeStruct(s0.shape, F32),
        jax.ShapeDtypeStruct(c0.shape, F32),
        jax.ShapeDtypeStruct((rows_all, GROUP_W), F32),
        jax.ShapeDtypeStruct((rows_all, GROUP_W), F32),
    ]
    return pl.pallas_call(
        functools.partial(_s_mix_kernel, seq=seq),
        grid=(ng, N_HEADS),
        in_specs=in_specs, out_specs=out_specs, out_shape=out_shape,
        scratch_shapes=[pltpu.VMEM((blk, HEAD), F32)],
        compiler_params=pltpu.CompilerParams(dimension_semantics=("arbitrary", "arbitrary")),
        name="sample_mix",
    )(proj, proj, proj, proj, proj, proj, proj, proj, proj, hist, hist, s0, c0, n_rows, m_rows, gbias,
      mcw, mcw, mcb, mcb, rnw, mnw, cos_t, sin_t, inner, qdec, kdec, cdec)


def _s_out_kernel(x_ref, yr_ref, ym_ref, g1_ref, sh2_ref, sc2_ref, g2_ref, n2_ref, nf_ref,
                  wout_ref, wupu_ref, wupv_ref, fcw_ref, fcb_ref, wdn_ref, fh_ref,
                  y_ref, u_ref, *, seq):
    rows = x_ref.shape[0]
    row_t = lax.broadcasted_iota(jnp.int32, (rows, D_FF), 0) & (seq - 1)
    mix = _dot(yr_ref[...], wout_ref[0:GROUP_W, :]) + _dot(ym_ref[...], wout_ref[GROUP_W:2 * GROUP_W, :])
    x1 = x_ref[...] + g1_ref[...] * mix

    def conv_fn(u):
        hist = fh_ref[...]
        uc = fcb_ref[...] + fcw_ref[FF_CONV - 1:FF_CONV, :] * u
        for d in range(1, FF_CONV):
            prev = jnp.where(row_t >= d, _roll_rows(u, d), _roll_rows(hist, d - (FF_CONV - 1)))
            uc = uc + fcw_ref[FF_CONV - 1 - d:FF_CONV - d, :] * prev
        return uc

    y, u = _ffn(x1, n2_ref[...] * (1.0 + sc2_ref[...]), sh2_ref[...], g2_ref[...], nf_ref[...],
                wupu_ref, wupv_ref, wdn_ref, fcw_ref, fcb_ref, conv_fn)
    y_ref[...] = y
    u_ref[...] = u


def _s_out_call(xs, yr, ym, mod_rows, n2, nf, wout, wupu, wupv, fcw, fcb, wdn, fhist, seq):
    rows_all = xs.shape[0]
    blk = S_OUT_ROWS
    nblk = rows_all // blk

    def modcol(cb):
        return pl.BlockSpec((blk, D_MODEL), lambda i, cb=cb: (i, cb))

    in_specs = [
        pl.BlockSpec((blk, D_MODEL), lambda i: (i, 0)),
        pl.BlockSpec((blk, GROUP_W), lambda i: (i, 0)),
        pl.BlockSpec((blk, GROUP_W), lambda i: (i, 0)),
        modcol(2), modcol(3), modcol(4), modcol(5),
        _const_spec((1, D_MODEL)), _const_spec((1, D_MODEL)),
        _const_spec((2 * GROUP_W, D_MODEL)),
        _const_spec((D_MODEL, D_FF)), _const_spec((D_MODEL, D_FF)),
        _const_spec((FF_CONV, D_FF)), _const_spec((1, D_FF)),
        _const_spec((D_FF, D_MODEL)),
        pl.BlockSpec((blk, D_FF), lambda i: (i, 0)),
    ]
    return pl.pallas_call(
        functools.partial(_s_out_kernel, seq=seq),
        grid=(nblk,),
        in_specs=in_specs,
        out_specs=[pl.BlockSpec((blk, D_MODEL), lambda i: (i, 0)),
                   pl.BlockSpec((blk, D_FF), lambda i: (i, 0))],
        out_shape=[jax.ShapeDtypeStruct((rows_all, D_MODEL), F32),
                   jax.ShapeDtypeStruct((rows_all, D_FF), F32)],
        compiler_params=pltpu.CompilerParams(
            dimension_semantics=("arbitrary",),
            vmem_limit_bytes=V7X_VMEM_BYTES - 4 * 1024 * 1024),
        name="sample_out",
    )(xs, yr, ym, mod_rows, mod_rows, mod_rows, mod_rows, n2, nf, wout, wupu, wupv, fcw, fcb, wdn, fhist)


def _rope_tables(pos):
    half = HEAD // 2
    inv = ROPE_THETA ** (-jnp.arange(half, dtype=F32) / half)
    ang = pos.astype(F32)[:, None] * inv[None, :]
    cos, sin = jnp.cos(ang), jnp.sin(ang)
    return jnp.concatenate([cos, cos], axis=-1), jnp.concatenate([-sin, sin], axis=-1)


def _retention_consts(cc, rows):
    lg = jnp.log1p(-jnp.exp2(-5.0 - jnp.arange(N_HEADS, dtype=F32)))
    ridx = jnp.arange(rows)
    t = (ridx % cc).astype(F32)
    rel = t[:, None] - t[None, :]
    same = (ridx[:, None] // cc) == (ridx[None, :] // cc)
    inner = jnp.where(same & (rel >= 0), jnp.exp(jnp.maximum(rel, 0.0) * lg[:, None, None]), 0.0)
    qdec = jnp.exp((t + 1.0) * lg[:, None])
    kdec = jnp.exp((cc - 1.0 - t) * lg[:, None])
    cdec = jnp.exp(cc * lg)
    wide = (N_HEADS, rows, HEAD)
    return (inner,
            jnp.broadcast_to(qdec[:, :, None], wide),
            jnp.broadcast_to(kdec[:, :, None], wide),
            jnp.broadcast_to(cdec[:, None, None], wide))


def kernel(x_prompt, x_sample, c_prompt, c_sample, state_ret, state_mlstm_C, state_mlstm_n, state_mlstm_m, state_mconv, state_ffconv, w_ada, b_ada, norm1_w, norm2_w, w_in, b_igate, b_fgate, mconv_w, mconv_b, ret_norm_w, mlstm_norm_w, w_out, w_up, ffconv_w, ffconv_b, w_down, final_w):
    assert w_ada.shape[0] == 1, "single-layer kernel"
    nb, seq_p, _ = x_prompt.shape
    ns, seq_s, _ = x_sample.shape
    assert seq_p % PROMPT_TILE == 0 and seq_s == 4 and ns % SAMPLE_GROUP == 0
    rows_s = ns * seq_s

    win = _bf(jnp.pad(w_in[0], ((0, 0), (0, IN_PAD - IN_COLS))))
    wout = _bf(w_out[0])
    wupu, wupv = _bf(w_up[0][:, :D_FF]), _bf(w_up[0][:, D_FF:])
    wdn = _bf(w_down[0])
    gbias = jnp.pad(jnp.concatenate([b_igate[0], b_fgate[0]]), (0, HEAD - 2 * N_HEADS))[None, :]
    n1, n2, nf = norm1_w, norm2_w, final_w[None, :]
    mcw, mcb = mconv_w[0], mconv_b
    rnw, mnw = ret_norm_w, mlstm_norm_w
    fcw, fcb = ffconv_w[0], ffconv_b

    mod = _ada_call(jnp.concatenate([c_prompt, c_sample], axis=0), w_ada[0], b_ada)
    mod_p = mod[:nb].reshape(nb, 6, D_MODEL)
    mod_s_rows = jnp.repeat(mod[nb:], seq_s, axis=0)

    cos_p, sin_p = _rope_tables(jnp.arange(seq_p, dtype=jnp.int32))
    y_p, ret_p, mc_p, mn_p, mm_p, mconv_t, ffconv_t = _prompt_call(
        x_prompt, mod_p, n1, n2, nf, win, gbias, mcw, mcb, rnw, mnw, wout, wupu, wupv, fcw, fcb, wdn,
        cos_p, sin_p, _retention_consts(CHUNK, CHUNK))
    mconv_p = mconv_t[:, HIST - (M_CONV - 1):, :]
    ffconv_p = ffconv_t[:, HIST - (FF_CONV - 1):, :]

    xs = x_sample.reshape(rows_s, D_MODEL)
    proj_s = _s_in_call(xs, mod_s_rows, n1, win)
    blk = SAMPLE_GROUP * seq_s
    cos_s, sin_s = _rope_tables(PAST_LEN + (jnp.arange(blk, dtype=jnp.int32) % seq_s))
    hist_m = jnp.pad(state_mconv[0], ((0, 0), (0, seq_s - (M_CONV - 1)), (0, 0))).reshape(rows_s, 2 * GROUP_W)
    n_rows = jnp.repeat(state_mlstm_n[0].reshape(ns, GROUP_W), seq_s, axis=0)
    m_rows = jnp.repeat(state_mlstm_m[0], seq_s, axis=0).T[:, :, None]
    yr, ym, ret_s, mc_s, n_new, m_new = _s_mix_call(
        proj_s, hist_m, state_ret[0], state_mlstm_C[0], n_rows, m_rows, gbias, mcw, mcb, rnw, mnw,
        cos_s, sin_s, _retention_consts(seq_s, blk), seq_s)
    hist_f = jnp.pad(state_ffconv[0], ((0, 0), (0, seq_s - (FF_CONV - 1)), (0, 0))).reshape(rows_s, D_FF)
    y_s, u_s = _s_out_call(xs, yr, ym, mod_s_rows, n2, nf, wout, wupu, wupv, fcw, fcb, wdn, hist_f, seq_s)

    mn_s = n_new.reshape(ns, seq_s, N_HEADS, HEAD)[:, 0]
    mm_s = m_new.reshape(ns, seq_s, N_HEADS, HEAD)[:, 0, :, 0]
    proj3 = proj_s.reshape(ns, seq_s, IN_PAD)
    mconv_s = proj3[:, seq_s - (M_CONV - 1):, C_MQ:C_MV]
    ffconv_s = u_s.reshape(ns, seq_s, D_FF)[:, seq_s - (FF_CONV - 1):, :]

    return (y_p, y_s.reshape(ns, seq_s, D_MODEL),
            ret_p[None], mc_p[None], mn_p[None], mm_p[:, :, 0][None], mconv_p[None], ffconv_p[None],
            ret_s[None], mc_s[None], mn_s[None], mm_s[None], mconv_s[None], ffconv_s[None])
```

```python
import functools

import jax
import jax.numpy as jnp
from jax import lax
from jax.experimental import pallas as pl
from jax.experimental.pallas import tpu as pltpu

F32 = jnp.float32
BF16 = jnp.bfloat16

D_MODEL = 1024
N_HEADS = 4
HEAD = 128
GROUP_W = N_HEADS * HEAD
D_FF = 2816
M_CONV = 4
FF_CONV = 3
CHUNK = 128
PAST_LEN = 16384
ROPE_THETA = 10000.0
RMS_EPS = 1e-6
GN_EPS = 1e-5
NEG = -1e30

P_MQ, P_MK, P_GATE = 0, GROUP_W, 2 * GROUP_W
P_HEAD0 = 2 * GROUP_W + HEAD
HEAD_COLS = 6 * HEAD
O_RQ, O_RK, O_RV, O_RG, O_MV, O_MO = (i * HEAD for i in range(6))
IN_PAD = P_HEAD0 + N_HEADS * HEAD_COLS
FFN_PIECE = 256
HIST = 8

V7X_VMEM_BYTES = 64 * 1024 * 1024
PROMPT_TILE = 256
SAMPLE_GROUP = 32
S_OUT_ROWS = 256


def _dot(a, b):
    return jnp.dot(a, b, preferred_element_type=F32)


def _dot_nt(a, b):
    return lax.dot_general(a, b, (((1,), (1,)), ((), ())), preferred_element_type=F32)


def _dot_tn(a, b):
    return lax.dot_general(a, b, (((0,), (0,)), ((), ())), preferred_element_type=F32)


def _bf(x):
    return x.astype(BF16)


def _silu(x):
    return x * (1.0 / (1.0 + jnp.exp(-x)))


def _sigmoid(x):
    return 1.0 / (1.0 + jnp.exp(-x))


def _log_sigmoid(x):
    return jnp.minimum(x, 0.0) - jnp.log(1.0 + jnp.exp(-jnp.abs(x)))


def _rms(x):
    return x * lax.rsqrt(jnp.mean(x * x, axis=-1, keepdims=True) + RMS_EPS)


def _head_norm(x, w_row):
    mu = jnp.mean(x, axis=-1, keepdims=True)
    xc = x - mu
    var = jnp.mean(xc * xc, axis=-1, keepdims=True)
    return xc * lax.rsqrt(var + GN_EPS) * w_row


def _rope(x, cos_t, sin_t):
    return x * cos_t + pltpu.roll(x, HEAD // 2, 1) * sin_t


def _pick_lane(x, idx):
    lane = lax.broadcasted_iota(jnp.int32, x.shape, 1)
    return jnp.sum(jnp.where(lane == idx, x, 0.0), axis=1, keepdims=True)


def _pick_row(x, idx):
    row = lax.broadcasted_iota(jnp.int32, x.shape, 0)
    return jnp.sum(jnp.where(row == idx, x, 0.0), axis=0, keepdims=True)


def _split3(x):
    hi = _bf(x)
    r = x - hi.astype(F32)
    mid = _bf(r)
    lo = _bf(r - mid.astype(F32))
    return hi, mid, lo


def _roll_rows(x, shift):
    shift = shift % x.shape[0]
    return x if shift == 0 else pltpu.roll(x, shift, 0)


class _ChunkOps:
    def __init__(self, rows, seq):
        self.rows, self.seq = rows, seq
        r = lax.broadcasted_iota(jnp.int32, (rows, rows), 0)
        c = lax.broadcasted_iota(jnp.int32, (rows, rows), 1)
        if seq == rows:
            self.causal = c <= r
            self.tril = jnp.where(self.causal, 1.0, 0.0).astype(BF16)
        else:
            shift = seq.bit_length() - 1
            same = (r >> shift) == (c >> shift)
            self.causal = same & (c <= r)
            self.t = lax.broadcasted_iota(jnp.int32, (rows, HEAD), 0) & (seq - 1)

    def cumsum(self, x):
        if self.seq == self.rows:
            hi, mid, lo = _split3(x)
            return _dot(self.tril, hi) + _dot(self.tril, mid) + _dot(self.tril, lo)
        out = x
        for d in range(1, self.seq):
            out = out + jnp.where(self.t >= d, _roll_rows(x, d), 0.0)
        return out

    def last(self, x):
        if self.seq == self.rows:
            return jnp.broadcast_to(x[self.rows - 1:self.rows, :], x.shape)
        y = jnp.where(self.t == self.seq - 1, x, 0.0)
        out = y
        for d in range(1, self.seq):
            out = out + _roll_rows(y, -d)
        return out

    def total(self, x):
        if self.seq == self.rows:
            return jnp.broadcast_to(jnp.sum(x, axis=0, keepdims=True), x.shape)
        assert self.seq == 4
        odd = (self.t & 1) == 1
        p = x + jnp.where(odd, _roll_rows(x, 1), _roll_rows(x, -1))
        return p + jnp.where(self.t >= 2, _roll_rows(p, 2), _roll_rows(p, -2))


def _repeat_rows(x, seq):
    n = x.shape[0]
    r = lax.broadcasted_iota(jnp.int32, (n * seq, n), 0)
    c = lax.broadcasted_iota(jnp.int32, (n * seq, n), 1)
    sel = jnp.where((r >> (seq.bit_length() - 1)) == c, 1.0, 0.0).astype(BF16)
    hi, mid, lo = _split3(x)
    return _dot(sel, hi) + _dot(sel, mid) + _dot(sel, lo)


def _gate_block(gates, gbias, ops):
    z = gates + gbias
    lane = lax.broadcasted_iota(jnp.int32, z.shape, 1)
    act = jnp.where(lane < N_HEADS, z, _log_sigmoid(z))
    csum = ops.cumsum(act)
    return act, csum, act.T, csum.T


def _ret_block(q, k, v, inner, qdec, kdec, state_read):
    s = _dot_nt(_bf(q), _bf(k)) * inner
    o = _dot(_bf(s), _bf(v)) + state_read(q * qdec)
    return o, k * kdec


def _mlstm_block(q, k, v, ig_col, ig_row, b_col, b_row, m_col, n_rows, ops, state_read):
    rows = q.shape[0]
    logw = jnp.where(ops.causal, b_col - b_row + ig_row, NEG)
    inter = b_col + m_col
    mt = jnp.maximum(inter, jnp.max(logw, axis=-1, keepdims=True))
    s = _dot_nt(_bf(q), _bf(k)) * jnp.exp(logw - mt)
    wi = jnp.exp(inter - mt)
    num = wi * state_read(q) + _dot(_bf(s), _bf(v))
    den = wi * jnp.sum(q * n_rows, axis=-1, keepdims=True) + jnp.sum(s, axis=-1, keepdims=True)
    h = num / jnp.maximum(jnp.abs(den), jnp.exp(-mt))
    b_last = ops.last(jnp.broadcast_to(b_col, (rows, HEAD)))
    m_new = ops.last(jnp.broadcast_to(mt, (rows, HEAD)))
    wk = jnp.exp(b_last - b_col + ig_col - m_new)
    wc = jnp.exp(b_last + m_col - m_new)
    kw = k * wk
    n_new = wc * n_rows + ops.total(kw)
    return h, kw, wc, m_new, n_new


def _ada_kernel(c_ref, w_ref, b_ref, o_ref):
    c = c_ref[...]
    o_ref[...] = _dot(_bf(_silu(c)), _bf(w_ref[...])) + b_ref[...]


def _ada_call(c_all, w_ada, b_ada):
    rows = c_all.shape[0]
    n_blk = w_ada.shape[1] // D_MODEL
    return pl.pallas_call(
        _ada_kernel,
        grid=(n_blk,),
        in_specs=[pl.BlockSpec((rows, D_MODEL), lambda n: (0, 0)),
                  pl.BlockSpec((D_MODEL, D_MODEL), lambda n: (0, n)),
                  pl.BlockSpec((1, D_MODEL), lambda n: (0, n))],
        out_specs=pl.BlockSpec((rows, D_MODEL), lambda n: (0, n)),
        out_shape=jax.ShapeDtypeStruct((rows, w_ada.shape[1]), F32),
        name="ada",
    )(c_all, w_ada, b_ada)


def _ffn(x1, a2, sh2, g2, nf, wup_ref, wdn_ref, conv_fn):
    h2 = _bf(_rms(x1) * a2 + sh2)
    u = _dot(h2, wup_ref[:, 0:D_FF])
    val = _dot(h2, wup_ref[:, D_FF:2 * D_FF])
    uc = conv_fn(u)
    act = _bf(_silu(uc) * val)
    x2 = x1 + g2 * _dot(act, wdn_ref[...])
    return _rms(x2) * nf, u


def _prompt_kernel(xa_ref, xc_ref, moda_ref, modc_ref, n1_ref, n2_ref, nf_ref, win_ref, gb_ref,
                   mcw_ref, mcb_ref, rnw_ref, mnw_ref, wout_ref, wup_ref, fcw_ref, fcb_ref, wdn_ref,
                   cos_ref, sin_ref, inner_ref, qdec_ref, kdec_ref, cdec_ref,
                   y_ref, ret_ref, mc_ref, mn_ref, mm_ref, mconv_ref, ffconv_ref,
                   proj_sc, qk_ext, qkc_sc, h_sc, u_ext, ymix_sc, x1_sc, s_sc, c_sc, n_sc, m_sc, *, tile, nt):
    i = pl.program_id(0)
    n_tiles = pl.num_programs(0) - 2
    jb = lax.rem(jnp.clip(i - 1, 0, n_tiles - 1), nt)
    jc = lax.rem(jnp.clip(i - 2, 0, n_tiles - 1), nt)

    @pl.when(i == 0)
    def _():
        proj_sc[...] = jnp.zeros_like(proj_sc)
        ymix_sc[...] = jnp.zeros_like(ymix_sc)

    @pl.when(jb == 0)
    def _():
        qk_ext[0:HIST, :] = jnp.zeros((HIST, 2 * GROUP_W), F32)
        s_sc[...] = jnp.zeros_like(s_sc)
        c_sc[...] = jnp.zeros_like(c_sc)
        n_sc[...] = jnp.zeros_like(n_sc)
        m_sc[...] = jnp.zeros_like(m_sc)

    @pl.when(jc == 0)
    def _():
        u_ext[0:HIST, :] = jnp.zeros((HIST, D_FF), F32)

    val = {}
    ops = _ChunkOps(CHUNK, CHUNK)

    def c_out():
        g1 = modc_ref[0, 2:3, :]
        sh2, sc2 = modc_ref[0, 3:4, :], modc_ref[0, 4:5, :]
        x1 = xc_ref[0] + g1 * _dot(ymix_sc[...], wout_ref[...])
        x1_sc[...] = x1
        val["h2"] = _bf(_rms(x1) * (n2_ref[...] * (1.0 + sc2)) + sh2)

    def c_ffn(k0, k1):
        def piece():
            h2 = val["h2"]
            u = _dot(h2, wup_ref[:, k0:k1])
            gate_in = _dot(h2, wup_ref[:, D_FF + k0:D_FF + k1])
            u_ext[HIST:HIST + tile, k0:k1] = u
            uc = fcb_ref[:, k0:k1] + fcw_ref[FF_CONV - 1:FF_CONV, k0:k1] * u
            for t in range(FF_CONV - 1):
                off = HIST - (FF_CONV - 1) + t
                uc = uc + fcw_ref[t:t + 1, k0:k1] * u_ext[off:off + tile, k0:k1]
            d = _dot(_bf(_silu(uc) * gate_in), wdn_ref[k0:k1, :])
            val["acc"] = d if "acc" not in val else val["acc"] + d
        return piece

    def c_final():
        g2 = modc_ref[0, 5:6, :]
        y_ref[0] = _rms(x1_sc[...] + g2 * val["acc"]) * nf_ref[...]
        u_ext[0:HIST, :] = u_ext[tile:tile + HIST, :]

    def a_norm():
        sh1, sc1 = moda_ref[0, 0:1, :], moda_ref[0, 1:2, :]
        h_sc[...] = _bf(_rms(xa_ref[0]) * (n1_ref[...] * (1.0 + sc1)) + sh1)

    def a_proj(c0, c1):
        def piece():
            proj_sc[:, c0:c1] = _dot(h_sc[...], win_ref[:, c0:c1])
        return piece

    def b_conv():
        qk_ext[HIST:HIST + tile, :] = proj_sc[:, P_MQ:P_GATE]
        conv = mcb_ref[...] + mcw_ref[M_CONV - 1:M_CONV, :] * qk_ext[HIST:HIST + tile, :]
        for t in range(M_CONV - 1):
            off = HIST - (M_CONV - 1) + t
            conv = conv + mcw_ref[t:t + 1, :] * qk_ext[off:off + tile, :]
        qkc_sc[...] = _silu(conv)
        qk_ext[0:HIST, :] = qk_ext[tile:tile + HIST, :]

    def b_gates(c):
        val["gate", c] = _gate_block(proj_sc[c * CHUNK:(c + 1) * CHUNK, P_GATE:P_HEAD0], gb_ref[...], ops)

    def head_unit(c, hd):
        rows = slice(c * CHUNK, (c + 1) * CHUNK)
        cols = slice(hd * HEAD, (hd + 1) * HEAD)
        base = P_HEAD0 + hd * HEAD_COLS

        def pcol(off):
            return proj_sc[rows, base + off:base + off + HEAD]

        cos_t, sin_t = cos_ref[rows, :], sin_ref[rows, :]
        q = _rope(pcol(O_RQ), cos_t, sin_t)
        k = _rope(pcol(O_RK), cos_t, sin_t) * (HEAD ** -0.5)
        v = pcol(O_RV)
        s_ret = _dot_nt(_bf(q), _bf(k))
        qs = _dot(_bf(q * qdec_ref[hd]), _bf(s_sc[hd]))
        mq = qkc_sc[rows, hd * HEAD:(hd + 1) * HEAD] * (HEAD ** -0.5)
        mk = qkc_sc[rows, GROUP_W + hd * HEAD:GROUP_W + (hd + 1) * HEAD]
        mv = pcol(O_MV)
        s_ml = _dot_nt(_bf(mq), _bf(mk))
        qc = _dot(_bf(mq), _bf(c_sc[hd]))
        out_gate_r = _silu(pcol(O_RG))
        out_gate_m = _sigmoid(pcol(O_MO))
        yield
        o = _dot(_bf(s_ret * inner_ref[hd]), _bf(v)) + qs
        s_upd = _dot_tn(_bf(k * kdec_ref[hd]), _bf(v))
        act, csum, act_t, csum_t = val["gate", c]
        ig_col, ig_row = _pick_lane(act, hd), _pick_row(act_t, hd)
        b_col, b_row = _pick_lane(csum, N_HEADS + hd), _pick_row(csum_t, N_HEADS + hd)
        m_col, n_rows = m_sc[hd][:, 0:1], n_sc[hd]
        logw = jnp.where(ops.causal, b_col - b_row + ig_row, NEG)
        inter = b_col + m_col
        mt = jnp.maximum(inter, jnp.max(logw, axis=-1, keepdims=True))
        sm = s_ml * jnp.exp(logw - mt)
        wi = jnp.exp(inter - mt)
        num = wi * qc + _dot(_bf(sm), _bf(mv))
        den = wi * jnp.sum(mq * n_rows, axis=-1, keepdims=True) + jnp.sum(sm, axis=-1, keepdims=True)
        b_last = ops.last(jnp.broadcast_to(b_col, (CHUNK, HEAD)))
        m_new = ops.last(jnp.broadcast_to(mt, (CHUNK, HEAD)))
        wk = jnp.exp(b_last - b_col + ig_col - m_new)
        wc = jnp.exp(b_last + m_col - m_new)
        kw = mk * wk
        c_upd = _dot_tn(_bf(kw), _bf(mv))
        yield
        s_sc[hd] = cdec_ref[hd] * s_sc[hd] + s_upd
        ymix_sc[rows, cols] = _bf(_head_norm(o, rnw_ref[:, cols]) * out_gate_r)
        c_sc[hd] = wc * c_sc[hd] + c_upd
        n_sc[hd] = wc * n_rows + ops.total(kw)
        m_sc[hd] = m_new
        hm = num / jnp.maximum(jnp.abs(den), jnp.exp(-mt))
        ymix_sc[rows, GROUP_W + hd * HEAD:GROUP_W + (hd + 1) * HEAD] = _bf(
            _head_norm(hm, mnw_ref[:, cols]) * out_gate_m)
        yield

    def ffn_unit(k0, k1):
        h2 = val["h2"]
        u = _dot(h2, wup_ref[:, k0:k1])
        gate_in = _dot(h2, wup_ref[:, D_FF + k0:D_FF + k1])
        yield
        u_ext[HIST:HIST + tile, k0:k1] = u
        uc = fcb_ref[:, k0:k1] + fcw_ref[FF_CONV - 1:FF_CONV, k0:k1] * u
        for t in range(FF_CONV - 1):
            off = HIST - (FF_CONV - 1) + t
            uc = uc + fcw_ref[t:t + 1, k0:k1] * u_ext[off:off + tile, k0:k1]
        d = _dot(_bf(_silu(uc) * gate_in), wdn_ref[k0:k1, :])
        val["acc"] = d if "acc" not in val else val["acc"] + d
        yield

    n_chunks = tile // CHUNK
    heads = [head_unit(c, hd) for c in range(n_chunks) for hd in range(N_HEADS)]
    proj_heads = [a_proj(P_HEAD0 + hd * HEAD_COLS, P_HEAD0 + (hd + 1) * HEAD_COLS) for hd in range(N_HEADS)]
    n_ffn = -(-D_FF // FFN_PIECE)
    ffn = None
    big = []
    for k in range(n_ffn + 1):
        big.append(("ffn", k))
    small = []
    a_norm()
    b_conv()
    for c in range(n_chunks):
        b_gates(c)
    c_out()
    ffn = [ffn_unit(k * FFN_PIECE, min((k + 1) * FFN_PIECE, D_FF)) for k in range(n_ffn)]

    def ffn_slot(k):
        if k < n_ffn:
            next(ffn[k])
        if k >= 1:
            next(ffn[k - 1])

    slots = [lambda k=k: ffn_slot(k) for k in range(n_ffn + 1)]
    slots += [a_proj(0, GROUP_W), a_proj(GROUP_W, P_HEAD0)]
    slot_iter = iter(slots)
    tail = list(proj_heads)
    for u, unit in enumerate(heads):
        next(unit)
        next(slot_iter, lambda: None)()
        next(unit)
        next(slot_iter, lambda: None)()
        next(unit)
        if u >= N_HEADS * (n_chunks - 1):
            pass
    for rest in slot_iter:
        rest()
    c_final()
    for piece in tail:
        piece()

    @pl.when((jb == nt - 1) & (i >= 1) & (i <= n_tiles))
    def _():
        mconv_ref[0] = qk_ext[0:HIST, :]
        ret_ref[0] = s_sc[...]
        mc_ref[0] = c_sc[...]
        for hd in range(N_HEADS):
            mn_ref[0, hd:hd + 1, :] = n_sc[hd][0:1, :]
            mm_ref[0, hd:hd + 1, :] = m_sc[hd][0:1, :]

    @pl.when((jc == nt - 1) & (i >= 2))
    def _():
        ffconv_ref[0] = u_ext[0:HIST, :]


def _const_spec(shape):
    nd = len(shape)
    return pl.BlockSpec(shape, lambda *_: (0,) * nd, pipeline_mode=pl.Buffered(1))


def _prompt_call(x, mod, n1, n2, nf, win, gbias, mcw, mcb, rnw, mnw, wout, wup, fcw, fcb, wdn,
                 cos_t, sin_t, consts):
    nb, seq, _ = x.shape
    tile = PROMPT_TILE
    nt = seq // tile
    n_tiles = nb * nt
    inner, qdec, kdec, cdec = consts
    state = (N_HEADS, HEAD, HEAD)

    def tile_a(i):
        return jnp.minimum(i, n_tiles - 1)

    def tile_b(i):
        return jnp.clip(i - 1, 0, n_tiles - 1)

    def tile_c(i):
        return jnp.clip(i - 2, 0, n_tiles - 1)

    in_specs = [
        pl.BlockSpec((1, tile, D_MODEL), lambda s: (tile_a(s) // nt, tile_a(s) % nt, 0)),
        pl.BlockSpec((1, tile, D_MODEL), lambda s: (tile_c(s) // nt, tile_c(s) % nt, 0)),
        pl.BlockSpec((1, 6, D_MODEL), lambda s: (tile_a(s) // nt, 0, 0)),
        pl.BlockSpec((1, 6, D_MODEL), lambda s: (tile_c(s) // nt, 0, 0)),
        _const_spec((1, D_MODEL)), _const_spec((1, D_MODEL)), _const_spec((1, D_MODEL)),
        _const_spec((D_MODEL, IN_PAD)), _const_spec((1, HEAD)),
        _const_spec((M_CONV, 2 * GROUP_W)), _const_spec((1, 2 * GROUP_W)),
        _const_spec((1, GROUP_W)), _const_spec((1, GROUP_W)),
        _const_spec((2 * GROUP_W, D_MODEL)),
        _const_spec((D_MODEL, 2 * D_FF)),
        _const_spec((FF_CONV, D_FF)), _const_spec((1, D_FF)),
        _const_spec((D_FF, D_MODEL)),
        pl.BlockSpec((tile, HEAD), lambda s: (tile_b(s) % nt, 0)),
        pl.BlockSpec((tile, HEAD), lambda s: (tile_b(s) % nt, 0)),
        _const_spec(state), _const_spec(state), _const_spec(state), _const_spec(state),
    ]
    out_specs = [
        pl.BlockSpec((1, tile, D_MODEL), lambda s: (tile_c(s) // nt, tile_c(s) % nt, 0)),
        pl.BlockSpec((1,) + state, lambda s: (tile_b(s) // nt, 0, 0, 0)),
        pl.BlockSpec((1,) + state, lambda s: (tile_b(s) // nt, 0, 0, 0)),
        pl.BlockSpec((1, N_HEADS, HEAD), lambda s: (tile_b(s) // nt, 0, 0)),
        pl.BlockSpec((1, N_HEADS, HEAD), lambda s: (tile_b(s) // nt, 0, 0)),
        pl.BlockSpec((1, HIST, 2 * GROUP_W), lambda s: (tile_b(s) // nt, 0, 0)),
        pl.BlockSpec((1, HIST, D_FF), lambda s: (tile_c(s) // nt, 0, 0)),
    ]
    out_shape = [
        jax.ShapeDtypeStruct((nb, seq, D_MODEL), F32),
        jax.ShapeDtypeStruct((nb,) + state, F32),
        jax.ShapeDtypeStruct((nb,) + state, F32),
        jax.ShapeDtypeStruct((nb, N_HEADS, HEAD), F32),
        jax.ShapeDtypeStruct((nb, N_HEADS, HEAD), F32),
        jax.ShapeDtypeStruct((nb, HIST, 2 * GROUP_W), F32),
        jax.ShapeDtypeStruct((nb, HIST, D_FF), F32),
    ]
    scratch = [
        pltpu.VMEM((tile, IN_PAD), F32),
        pltpu.VMEM((tile + HIST, 2 * GROUP_W), F32),
        pltpu.VMEM((tile, 2 * GROUP_W), F32),
        pltpu.VMEM((tile, D_MODEL), BF16),
        pltpu.VMEM((tile + HIST, D_FF), F32),
        pltpu.VMEM((tile, 2 * GROUP_W), BF16),
        pltpu.VMEM((tile, D_MODEL), F32),
        pltpu.VMEM(state, F32), pltpu.VMEM(state, F32), pltpu.VMEM(state, F32), pltpu.VMEM(state, F32),
    ]
    return pl.pallas_call(
        functools.partial(_prompt_kernel, tile=tile, nt=nt),
        grid=(n_tiles + 2,),
        in_specs=in_specs, out_specs=out_specs, out_shape=out_shape, scratch_shapes=scratch,
        compiler_params=pltpu.CompilerParams(
            dimension_semantics=("arbitrary",),
            vmem_limit_bytes=V7X_VMEM_BYTES - 4 * 1024 * 1024),
        name="prompt_layer",
    )(x, x, mod, mod, n1, n2, nf, win, gbias, mcw, mcb, rnw, mnw, wout, wup, fcw, fcb, wdn,
      cos_t, sin_t, inner, qdec, kdec, cdec)


def _s_in_kernel(x_ref, sh1_ref, sc1_ref, n1_ref, win_ref, o_ref, *, seq):
    a1 = _repeat_rows(n1_ref[...] * (1.0 + sc1_ref[...]), seq)
    h = _bf(_rms(x_ref[...]) * a1 + _repeat_rows(sh1_ref[...], seq))
    o_ref[...] = _dot(h, win_ref[...])


def _s_in_call(xs, mod_s, n1, win, seq):
    rows = xs.shape[0]
    nbatch = rows // seq
    nblk = 3
    wblk = IN_PAD // nblk
    return pl.pallas_call(
        functools.partial(_s_in_kernel, seq=seq),
        grid=(nblk,),
        in_specs=[pl.BlockSpec((rows, D_MODEL), lambda n: (0, 0)),
                  pl.BlockSpec((nbatch, D_MODEL), lambda n: (0, 0)),
                  pl.BlockSpec((nbatch, D_MODEL), lambda n: (0, 1)),
                  pl.BlockSpec((1, D_MODEL), lambda n: (0, 0)),
                  pl.BlockSpec((D_MODEL, wblk), lambda n: (0, n))],
        out_specs=pl.BlockSpec((rows, wblk), lambda n: (0, n)),
        out_shape=jax.ShapeDtypeStruct((rows, IN_PAD), F32),
        name="sample_in",
    )(xs, mod_s, mod_s, n1, win)


def _s_mix_kernel(rq_ref, rk_ref, rv_ref, rg_ref, mq_ref, mk_ref, mv_ref, mo_ref, gate_ref,
                  hq_ref, hk_ref, s0_ref, c0_ref, n0_ref, m0_ref, gb_ref,
                  mcwq_ref, mcwk_ref, mcbq_ref, mcbk_ref, rnw_ref, mnw_ref,
                  cos_ref, sin_ref, inner_ref, qdec_ref, kdec_ref, cdec_ref,
                  yr_ref, ym_ref, s1_ref, c1_ref, n1_ref, m1_ref, o_sc, *, seq):
    hd = pl.program_id(1)
    rows = rq_ref.shape[0]
    nbatch = rows // seq
    ops = _ChunkOps(rows, seq)
    row_t = lax.broadcasted_iota(jnp.int32, (rows, HEAD), 0) & (seq - 1)
    tile_half = lax.broadcasted_iota(jnp.int32, (2 * seq, HEAD), 0) >= seq
    lane_b = lax.broadcasted_iota(jnp.int32, (HEAD, rows), 1) >> (seq.bit_length() - 1)

    def state_read(state_ref):
        def read(qq):
            for i in range(nbatch // 2):
                qt = qq[2 * seq * i:2 * seq * (i + 1), :]
                lo = _dot(_bf(jnp.where(tile_half, 0.0, qt)), _bf(state_ref[2 * i, 0]))
                hi = _dot(_bf(jnp.where(tile_half, qt, 0.0)), _bf(state_ref[2 * i + 1, 0]))
                o_sc[2 * seq * i:2 * seq * (i + 1), :] = lo + hi
            return o_sc[...]
        return read

    def state_write(new_ref, old_ref, decay_rows, kx, v):
        kx_t = kx.T
        vb = _bf(v)
        for b in range(nbatch):
            upd = _dot(_bf(jnp.where(lane_b == b, kx_t, 0.0)), vb)
            dec = jnp.broadcast_to(decay_rows[seq * b:seq * b + 1, :], (HEAD, HEAD))
            new_ref[b, 0] = dec * old_ref[b, 0] + upd

    cos_t, sin_t = cos_ref[...], sin_ref[...]
    q = _rope(rq_ref[...], cos_t, sin_t)
    k = _rope(rk_ref[...], cos_t, sin_t) * (HEAD ** -0.5)
    v = rv_ref[...]
    o, kd = _ret_block(q, k, v, inner_ref[0], qdec_ref[0], kdec_ref[0], state_read(s0_ref))
    state_write(s1_ref, s0_ref, cdec_ref[0], kd, v)
    yr_ref[...] = _bf(_head_norm(o, rnw_ref[...]) * _silu(rg_ref[...]))

    def conv(x, hist, w_ref, b_ref):
        out = b_ref[...] + w_ref[M_CONV - 1:M_CONV, :] * x
        for d in range(1, M_CONV):
            prev = jnp.where(row_t >= d, _roll_rows(x, d), _roll_rows(hist, d - (M_CONV - 1)))
            out = out + w_ref[M_CONV - 1 - d:M_CONV - d, :] * prev
        return out

    mq = _silu(conv(mq_ref[...], hq_ref[...], mcwq_ref, mcbq_ref)) * (HEAD ** -0.5)
    mk = _silu(conv(mk_ref[...], hk_ref[...], mcwk_ref, mcbk_ref))
    mv = mv_ref[...]
    act, csum, act_t, csum_t = _gate_block(gate_ref[...], gb_ref[...], ops)
    m_col = m0_ref[0]
    hm, kw, wc, m_new, n_new = _mlstm_block(
        mq, mk, mv,
        _pick_lane(act, hd), _pick_row(act_t, hd),
        _pick_lane(csum, N_HEADS + hd), _pick_row(csum_t, N_HEADS + hd),
        m_col, n0_ref[...], ops, state_read(c0_ref))
    state_write(c1_ref, c0_ref, wc, kw, mv)
    n1_ref[...] = n_new
    m1_ref[...] = m_new
    ym_ref[...] = _bf(_head_norm(hm, mnw_ref[...]) * _sigmoid(mo_ref[...]))


def _s_mix_call(proj, hist, s0, c0, n_rows, m_rows, gbias, mcw, mcb, rnw, mnw, cos_t, sin_t, consts, seq):
    rows_all = proj.shape[0]
    blk = SAMPLE_GROUP * seq
    ng = rows_all // blk
    inner, qdec, kdec, cdec = consts

    def col(cb):
        return pl.BlockSpec((blk, HEAD), lambda g, h, cb=cb: (g, cb + h))

    def hcol(off):
        return pl.BlockSpec(
            (blk, HEAD), lambda g, h, off=off: (g, (P_HEAD0 + off) // HEAD + h * (HEAD_COLS // HEAD)))

    def head_const(arr_rows):
        return pl.BlockSpec((arr_rows, HEAD), lambda g, h: (0, h))

    st_spec = pl.BlockSpec((SAMPLE_GROUP, 1, HEAD, HEAD), lambda g, h: (g, h, 0, 0))
    hconst = pl.BlockSpec((1, HEAD, HEAD), lambda g, h: (h, 0, 0))
    in_specs = [
        hcol(O_RQ), hcol(O_RK), hcol(O_RV), hcol(O_RG),
        col(P_MQ // HEAD), col(P_MK // HEAD), hcol(O_MV), hcol(O_MO),
        pl.BlockSpec((blk, HEAD), lambda g, h: (g, P_GATE // HEAD)),
        col(0), col(N_HEADS),
        st_spec, st_spec,
        pl.BlockSpec((blk, HEAD), lambda g, h: (g, h)),
        pl.BlockSpec((1, blk, 1), lambda g, h: (h, g, 0)),
        pl.BlockSpec((1, HEAD), lambda g, h: (0, 0)),
        head_const(M_CONV), pl.BlockSpec((M_CONV, HEAD), lambda g, h: (0, N_HEADS + h)),
        head_const(1), pl.BlockSpec((1, HEAD), lambda g, h: (0, N_HEADS + h)),
        head_const(1), head_const(1),
        pl.BlockSpec((blk, HEAD), lambda g, h: (0, 0)),
        pl.BlockSpec((blk, HEAD), lambda g, h: (0, 0)),
        hconst, hconst, hconst, hconst,
    ]
    out_specs = [
        pl.BlockSpec((blk, HEAD), lambda g, h: (g, h)),
        pl.BlockSpec((blk, HEAD), lambda g, h: (g, h)),
        st_spec, st_spec,
        pl.BlockSpec((blk, HEAD), lambda g, h: (g, h)),
        pl.BlockSpec((blk, HEAD), lambda g, h: (g, h)),
    ]
    out_shape = [
        jax.ShapeDtypeStruct((rows_all, GROUP_W), BF16),
        jax.ShapeDtypeStruct((rows_all, GROUP_W), BF16),
        jax.ShapeDtypeStruct(s0.shape, F32),
        jax.ShapeDtypeStruct(c0.shape, F32),
        jax.ShapeDtypeStruct((rows_all, GROUP_W), F32),
        jax.ShapeDtypeStruct((rows_all, GROUP_W), F32),
    ]
    return pl.pallas_call(
        functools.partial(_s_mix_kernel, seq=seq),
        grid=(ng, N_HEADS),
        in_specs=in_specs, out_specs=out_specs, out_shape=out_shape,
        scratch_shapes=[pltpu.VMEM((blk, HEAD), F32)],
        compiler_params=pltpu.CompilerParams(dimension_semantics=("arbitrary", "arbitrary")),
        name="sample_mix",
    )(proj, proj, proj, proj, proj, proj, proj, proj, proj, hist, hist, s0, c0, n_rows, m_rows, gbias,
      mcw, mcw, mcb, mcb, rnw, mnw, cos_t, sin_t, inner, qdec, kdec, cdec)


def _s_out_kernel(x_ref, yr_ref, ym_ref, g1_ref, sh2_ref, sc2_ref, g2_ref, n2_ref, nf_ref,
                  wout_ref, wup_ref, fcw_ref, fcb_ref, wdn_ref, fh_ref,
                  y_ref, u_ref, *, seq):
    rows = x_ref.shape[0]
    row_t = lax.broadcasted_iota(jnp.int32, (rows, D_FF), 0) & (seq - 1)
    mix = _dot(yr_ref[...], wout_ref[0:GROUP_W, :]) + _dot(ym_ref[...], wout_ref[GROUP_W:2 * GROUP_W, :])
    x1 = x_ref[...] + _repeat_rows(g1_ref[...], seq) * mix

    def conv_fn(u):
        hist = fh_ref[...]
        uc = fcb_ref[...] + fcw_ref[FF_CONV - 1:FF_CONV, :] * u
        for d in range(1, FF_CONV):
            prev = jnp.where(row_t >= d, _roll_rows(u, d), _roll_rows(hist, d - (FF_CONV - 1)))
            uc = uc + fcw_ref[FF_CONV - 1 - d:FF_CONV - d, :] * prev
        return uc

    y, u = _ffn(x1, _repeat_rows(n2_ref[...] * (1.0 + sc2_ref[...]), seq), _repeat_rows(sh2_ref[...], seq),
                _repeat_rows(g2_ref[...], seq), nf_ref[...], wup_ref, wdn_ref, conv_fn)
    y_ref[...] = y
    u_ref[...] = u


def _s_out_call(xs, yr, ym, mod_s, n2, nf, wout, wup, fcw, fcb, wdn, fhist, seq):
    rows_all = xs.shape[0]
    blk = S_OUT_ROWS
    nblk = rows_all // blk

    def modcol(cb):
        return pl.BlockSpec((blk // seq, D_MODEL), lambda i, cb=cb: (i, cb))

    in_specs = [
        pl.BlockSpec((blk, D_MODEL), lambda i: (i, 0)),
        pl.BlockSpec((blk, GROUP_W), lambda i: (i, 0)),
        pl.BlockSpec((blk, GROUP_W), lambda i: (i, 0)),
        modcol(2), modcol(3), modcol(4), modcol(5),
        _const_spec((1, D_MODEL)), _const_spec((1, D_MODEL)),
        _const_spec((2 * GROUP_W, D_MODEL)),
        _const_spec((D_MODEL, 2 * D_FF)),
        _const_spec((FF_CONV, D_FF)), _const_spec((1, D_FF)),
        _const_spec((D_FF, D_MODEL)),
        pl.BlockSpec((blk, D_FF), lambda i: (i, 0)),
    ]
    return pl.pallas_call(
        functools.partial(_s_out_kernel, seq=seq),
        grid=(nblk,),
        in_specs=in_specs,
        out_specs=[pl.BlockSpec((blk, D_MODEL), lambda i: (i, 0)),
                   pl.BlockSpec((blk, D_FF), lambda i: (i, 0))],
        out_shape=[jax.ShapeDtypeStruct((rows_all, D_MODEL), F32),
                   jax.ShapeDtypeStruct((rows_all, D_FF), F32)],
        compiler_params=pltpu.CompilerParams(
            dimension_semantics=("arbitrary",),
            vmem_limit_bytes=V7X_VMEM_BYTES - 4 * 1024 * 1024),
        name="sample_out",
    )(xs, yr, ym, mod_s, mod_s, mod_s, mod_s, n2, nf, wout, wup, fcw, fcb, wdn, fhist)


def _rope_tables(pos):
    half = HEAD // 2
    inv = ROPE_THETA ** (-jnp.arange(half, dtype=F32) / half)
    ang = pos.astype(F32)[:, None] * inv[None, :]
    cos, sin = jnp.cos(ang), jnp.sin(ang)
    return jnp.concatenate([cos, cos], axis=-1), jnp.concatenate([-sin, sin], axis=-1)


def _retention_consts(cc, rows):
    lg = jnp.log1p(-jnp.exp2(-5.0 - jnp.arange(N_HEADS, dtype=F32)))
    ridx = jnp.arange(rows)
    t = (ridx % cc).astype(F32)
    rel = t[:, None] - t[None, :]
    same = (ridx[:, None] // cc) == (ridx[None, :] // cc)
    inner = jnp.where(same & (rel >= 0), jnp.exp(jnp.maximum(rel, 0.0) * lg[:, None, None]), 0.0)
    qdec = jnp.exp((t + 1.0) * lg[:, None])
    kdec = jnp.exp((cc - 1.0 - t) * lg[:, None])
    cdec = jnp.exp(cc * lg)
    wide = (N_HEADS, rows, HEAD)
    return (inner,
            jnp.broadcast_to(qdec[:, :, None], wide),
            jnp.broadcast_to(kdec[:, :, None], wide),
            jnp.broadcast_to(cdec[:, None, None], wide))


def _regroup_w_in(w):
    wb = _bf(w)
    gw = GROUP_W
    rq, rk, rv, rg, mq, mk, mv, mo = (wb[:, i * gw:(i + 1) * gw] for i in range(8))
    gates = jnp.pad(wb[:, 8 * gw:], ((0, 0), (0, HEAD - 2 * N_HEADS)))
    parts = [mq, mk, gates]
    for hd in range(N_HEADS):
        parts += [m[:, hd * HEAD:(hd + 1) * HEAD] for m in (rq, rk, rv, rg, mv, mo)]
    return jnp.concatenate(parts, axis=1)


def kernel(x_prompt, x_sample, c_prompt, c_sample, state_ret, state_mlstm_C, state_mlstm_n, state_mlstm_m, state_mconv, state_ffconv, w_ada, b_ada, norm1_w, norm2_w, w_in, b_igate, b_fgate, mconv_w, mconv_b, ret_norm_w, mlstm_norm_w, w_out, w_up, ffconv_w, ffconv_b, w_down, final_w):
    assert w_ada.shape[0] == 1, "single-layer kernel"
    nb, seq_p, _ = x_prompt.shape
    ns, seq_s, _ = x_sample.shape
    assert seq_p % PROMPT_TILE == 0 and seq_s == 4 and ns % SAMPLE_GROUP == 0
    rows_s = ns * seq_s

    win = _regroup_w_in(w_in[0])
    wout = _bf(w_out[0])
    wup = _bf(w_up[0])
    wdn = _bf(w_down[0])
    gbias = jnp.pad(jnp.concatenate([b_igate[0], b_fgate[0]]), (0, HEAD - 2 * N_HEADS))[None, :]
    n1, n2, nf = norm1_w, norm2_w, final_w[None, :]
    mcw, mcb = mconv_w[0], mconv_b
    rnw, mnw = ret_norm_w, mlstm_norm_w
    fcw, fcb = ffconv_w[0], ffconv_b

    mod = _ada_call(jnp.concatenate([c_prompt, c_sample], axis=0), w_ada[0], b_ada)
    mod_p = mod[:nb].reshape(nb, 6, D_MODEL)
    mod_s = mod[nb:]

    cos_p, sin_p = _rope_tables(jnp.arange(seq_p, dtype=jnp.int32))
    y_p, ret_p, mc_p, mn_p, mm_p, mconv_t, ffconv_t = _prompt_call(
        x_prompt, mod_p, n1, n2, nf, win, gbias, mcw, mcb, rnw, mnw, wout, wup, fcw, fcb, wdn,
        cos_p, sin_p, _retention_consts(CHUNK, CHUNK))
    mconv_p = mconv_t[:, HIST - (M_CONV - 1):, :]
    ffconv_p = ffconv_t[:, HIST - (FF_CONV - 1):, :]

    xs = x_sample.reshape(rows_s, D_MODEL)
    proj_s = _s_in_call(xs, mod_s, n1, win, seq_s)
    blk = SAMPLE_GROUP * seq_s
    cos_s, sin_s = _rope_tables(PAST_LEN + (jnp.arange(blk, dtype=jnp.int32) % seq_s))
    hist_m = jnp.pad(state_mconv[0], ((0, 0), (0, seq_s - (M_CONV - 1)), (0, 0))).reshape(rows_s, 2 * GROUP_W)
    n_rows = jnp.repeat(state_mlstm_n[0].reshape(ns, GROUP_W), seq_s, axis=0)
    m_rows = jnp.repeat(state_mlstm_m[0], seq_s, axis=0).T[:, :, None]
    yr, ym, ret_s, mc_s, n_new, m_new = _s_mix_call(
        proj_s, hist_m, state_ret[0], state_mlstm_C[0], n_rows, m_rows, gbias, mcw, mcb, rnw, mnw,
        cos_s, sin_s, _retention_consts(seq_s, blk), seq_s)
    hist_f = jnp.pad(state_ffconv[0], ((0, 0), (0, seq_s - (FF_CONV - 1)), (0, 0))).reshape(rows_s, D_FF)
    y_s, u_s = _s_out_call(xs, yr, ym, mod_s, n2, nf, wout, wup, fcw, fcb, wdn, hist_f, seq_s)

    mn_s = n_new[::seq_s].reshape(ns, N_HEADS, HEAD)
    mm_s = m_new[::seq_s].reshape(ns, N_HEADS, HEAD)[:, :, 0]
    mconv_s = jnp.stack([proj_s[t::seq_s, P_MQ:P_GATE] for t in range(seq_s - (M_CONV - 1), seq_s)], axis=1)
    ffconv_s = jnp.stack([u_s[t::seq_s, :] for t in range(seq_s - (FF_CONV - 1), seq_s)], axis=1)

    return (y_p, y_s.reshape(ns, seq_s, D_MODEL),
            ret_p[None], mc_p[None], mn_p[None], mm_p[:, :, 0][None], mconv_p[None], ffconv_p[None],
            ret_s[None], mc_s[None], mn_s[None], mm_s[None], mconv_s[None], ffconv_s[None])
```

```python
import functools

import jax
import jax.numpy as jnp
import numpy as np
from jax import lax
from jax.experimental import pallas as pl
from jax.experimental.pallas import tpu as pltpu

F32 = jnp.float32
BF16 = jnp.bfloat16

D_MODEL = 1024
N_HEADS = 4
HEAD = 128
GROUP_W = N_HEADS * HEAD
D_FF = 2816
M_CONV = 4
FF_CONV = 3
CHUNK = 128
PAST_LEN = 16384
ROPE_THETA = 10000.0
RMS_EPS = 1e-6
GN_EPS = 1e-5
NEG = -1e30

P_MQ, P_MK, P_GATE = 0, GROUP_W, 2 * GROUP_W
P_HEAD0 = 2 * GROUP_W + HEAD
HEAD_COLS = 6 * HEAD
O_RQ, O_RK, O_RV, O_RG, O_MV, O_MO = (i * HEAD for i in range(6))
IN_PAD = P_HEAD0 + N_HEADS * HEAD_COLS
FFN_PIECE = 256
HIST = 8

V7X_VMEM_BYTES = 64 * 1024 * 1024
PROMPT_TILE = 256
SAMPLE_GROUP = 32
S_OUT_ROWS = 256


def _dot(a, b):
    return jnp.dot(a, b, preferred_element_type=F32)


def _dot_nt(a, b):
    return lax.dot_general(a, b, (((1,), (1,)), ((), ())), preferred_element_type=F32)


def _dot_tn(a, b):
    return lax.dot_general(a, b, (((0,), (0,)), ((), ())), preferred_element_type=F32)


def _bf(x):
    return x.astype(BF16)


def _silu(x):
    return x * (1.0 / (1.0 + jnp.exp(-x)))


def _sigmoid(x):
    return 1.0 / (1.0 + jnp.exp(-x))


def _log_sigmoid(x):
    return jnp.minimum(x, 0.0) - jnp.log(1.0 + jnp.exp(-jnp.abs(x)))


def _rms(x):
    return x * lax.rsqrt(jnp.mean(x * x, axis=-1, keepdims=True) + RMS_EPS)


def _head_norm(x, w_row):
    mu = jnp.mean(x, axis=-1, keepdims=True)
    xc = x - mu
    var = jnp.mean(xc * xc, axis=-1, keepdims=True)
    return xc * lax.rsqrt(var + GN_EPS) * w_row


def _rope(x, cos_t, sin_t):
    return x * cos_t + pltpu.roll(x, HEAD // 2, 1) * sin_t


def _pick_lane(x, idx):
    lane = lax.broadcasted_iota(jnp.int32, x.shape, 1)
    return jnp.sum(jnp.where(lane == idx, x, 0.0), axis=1, keepdims=True)


def _pick_row(x, idx):
    row = lax.broadcasted_iota(jnp.int32, x.shape, 0)
    return jnp.sum(jnp.where(row == idx, x, 0.0), axis=0, keepdims=True)


def _split3(x):
    hi = _bf(x)
    r = x - hi.astype(F32)
    mid = _bf(r)
    lo = _bf(r - mid.astype(F32))
    return hi, mid, lo


def _roll_rows(x, shift):
    shift = shift % x.shape[0]
    return x if shift == 0 else pltpu.roll(x, shift, 0)


class _ChunkOps:
    def __init__(self, rows, seq):
        self.rows, self.seq = rows, seq
        r = lax.broadcasted_iota(jnp.int32, (rows, rows), 0)
        c = lax.broadcasted_iota(jnp.int32, (rows, rows), 1)
        if seq == rows:
            self.causal = c <= r
            self.tril = jnp.where(self.causal, 1.0, 0.0).astype(BF16)
        else:
            shift = seq.bit_length() - 1
            same = (r >> shift) == (c >> shift)
            self.causal = same & (c <= r)
            self.t = lax.broadcasted_iota(jnp.int32, (rows, HEAD), 0) & (seq - 1)

    def cumsum(self, x):
        if self.seq == self.rows:
            hi, mid, lo = _split3(x)
            return _dot(self.tril, hi) + _dot(self.tril, mid) + _dot(self.tril, lo)
        out = x
        for d in range(1, self.seq):
            out = out + jnp.where(self.t >= d, _roll_rows(x, d), 0.0)
        return out

    def last(self, x):
        if self.seq == self.rows:
            return jnp.broadcast_to(x[self.rows - 1:self.rows, :], x.shape)
        y = jnp.where(self.t == self.seq - 1, x, 0.0)
        out = y
        for d in range(1, self.seq):
            out = out + _roll_rows(y, -d)
        return out

    def total(self, x):
        if self.seq == self.rows:
            return jnp.broadcast_to(jnp.sum(x, axis=0, keepdims=True), x.shape)
        assert self.seq == 4
        odd = (self.t & 1) == 1
        p = x + jnp.where(odd, _roll_rows(x, 1), _roll_rows(x, -1))
        return p + jnp.where(self.t >= 2, _roll_rows(p, 2), _roll_rows(p, -2))


def _repeat_rows(x, seq):
    n = x.shape[0]
    r = lax.broadcasted_iota(jnp.int32, (n * seq, n), 0)
    c = lax.broadcasted_iota(jnp.int32, (n * seq, n), 1)
    sel = jnp.where((r >> (seq.bit_length() - 1)) == c, 1.0, 0.0).astype(BF16)
    hi, mid, lo = _split3(x)
    return _dot(sel, hi) + _dot(sel, mid) + _dot(sel, lo)


def _gate_block(gates, gbias, ops):
    z = gates + gbias
    lane = lax.broadcasted_iota(jnp.int32, z.shape, 1)
    act = jnp.where(lane < N_HEADS, z, _log_sigmoid(z))
    csum = ops.cumsum(act)
    return act, csum, act.T, csum.T


def _ret_block(q, k, v, inner, qdec, kdec, state_read):
    s = _dot_nt(_bf(q), _bf(k)) * inner
    o = _dot(_bf(s), _bf(v)) + state_read(q * qdec)
    return o, k * kdec


def _mlstm_block(q, k, v, ig_col, ig_row, b_col, b_row, m_col, n_rows, ops, state_read):
    rows = q.shape[0]
    logw = jnp.where(ops.causal, b_col - b_row + ig_row, NEG)
    inter = b_col + m_col
    mt = jnp.maximum(inter, jnp.max(logw, axis=-1, keepdims=True))
    s = _dot_nt(_bf(q), _bf(k)) * jnp.exp(logw - mt)
    wi = jnp.exp(inter - mt)
    num = wi * state_read(q) + _dot(_bf(s), _bf(v))
    den = wi * jnp.sum(q * n_rows, axis=-1, keepdims=True) + jnp.sum(s, axis=-1, keepdims=True)
    h = num / jnp.maximum(jnp.abs(den), jnp.exp(-mt))
    b_last = ops.last(jnp.broadcast_to(b_col, (rows, HEAD)))
    m_new = ops.last(jnp.broadcast_to(mt, (rows, HEAD)))
    wk = jnp.exp(b_last - b_col + ig_col - m_new)
    wc = jnp.exp(b_last + m_col - m_new)
    kw = k * wk
    n_new = wc * n_rows + ops.total(kw)
    return h, kw, wc, m_new, n_new


def _ada_kernel(c_ref, w_ref, b_ref, o_ref):
    c = c_ref[...]
    o_ref[...] = _dot(_bf(_silu(c)), _bf(w_ref[...])) + b_ref[...]


def _ada_call(c_all, w_ada, b_ada):
    rows = c_all.shape[0]
    n_blk = w_ada.shape[1] // D_MODEL
    return pl.pallas_call(
        _ada_kernel,
        grid=(n_blk,),
        in_specs=[pl.BlockSpec((rows, D_MODEL), lambda n: (0, 0)),
                  pl.BlockSpec((D_MODEL, D_MODEL), lambda n: (0, n)),
                  pl.BlockSpec((1, D_MODEL), lambda n: (0, n))],
        out_specs=pl.BlockSpec((rows, D_MODEL), lambda n: (0, n)),
        out_shape=jax.ShapeDtypeStruct((rows, w_ada.shape[1]), F32),
        name="ada",
    )(c_all, w_ada, b_ada)


def _ffn(x1, a2, sh2, g2, nf, wup_ref, wdn_ref, conv_fn):
    h2 = _bf(_rms(x1) * a2 + sh2)
    u = _dot(h2, wup_ref[:, 0:D_FF])
    val = _dot(h2, wup_ref[:, D_FF:2 * D_FF])
    uc = conv_fn(u)
    act = _bf(_silu(uc) * val)
    x2 = x1 + g2 * _dot(act, wdn_ref[...])
    return _rms(x2) * nf, u


def _prompt_kernel(xa_ref, xc_ref, moda_ref, modc_ref, n1_ref, n2_ref, nf_ref, win_ref, gb_ref,
                   mcw_ref, mcb_ref, rnw_ref, mnw_ref, wout_ref, wup_ref, fcw_ref, fcb_ref, wdn_ref,
                   cos_ref, sin_ref, inner_ref, qdec_ref, kdec_ref, cdec_ref,
                   y_ref, ret_ref, mc_ref, mn_ref, mm_ref, mconv_ref, ffconv_ref,
                   proj_sc, qk_ext, qkc_sc, h_sc, u_ext, ymix_sc, x1_sc, s_sc, c_sc, n_sc, m_sc, *, tile, nt):
    i = pl.program_id(0)
    n_tiles = pl.num_programs(0) - 2
    jb = lax.rem(jnp.clip(i - 1, 0, n_tiles - 1), nt)
    jc = lax.rem(jnp.clip(i - 2, 0, n_tiles - 1), nt)

    @pl.when(i == 0)
    def _():
        proj_sc[...] = jnp.zeros_like(proj_sc)
        ymix_sc[...] = jnp.zeros_like(ymix_sc)

    @pl.when(jb == 0)
    def _():
        qk_ext[0:HIST, :] = jnp.zeros((HIST, 2 * GROUP_W), F32)
        s_sc[...] = jnp.zeros_like(s_sc)
        c_sc[...] = jnp.zeros_like(c_sc)
        n_sc[...] = jnp.zeros_like(n_sc)
        m_sc[...] = jnp.zeros_like(m_sc)

    @pl.when(jc == 0)
    def _():
        u_ext[0:HIST, :] = jnp.zeros((HIST, D_FF), F32)

    val = {}
    ops = _ChunkOps(CHUNK, CHUNK)

    def c_out():
        g1 = modc_ref[0, 2:3, :]
        sh2, sc2 = modc_ref[0, 3:4, :], modc_ref[0, 4:5, :]
        x1 = xc_ref[0] + g1 * _dot(ymix_sc[...], wout_ref[...])
        x1_sc[...] = x1
        val["h2"] = _bf(_rms(x1) * (n2_ref[...] * (1.0 + sc2)) + sh2)

    def c_ffn(k0, k1):
        def piece():
            h2 = val["h2"]
            u = _dot(h2, wup_ref[:, k0:k1])
            gate_in = _dot(h2, wup_ref[:, D_FF + k0:D_FF + k1])
            u_ext[HIST:HIST + tile, k0:k1] = u
            uc = fcb_ref[:, k0:k1] + fcw_ref[FF_CONV - 1:FF_CONV, k0:k1] * u
            for t in range(FF_CONV - 1):
                off = HIST - (FF_CONV - 1) + t
                uc = uc + fcw_ref[t:t + 1, k0:k1] * u_ext[off:off + tile, k0:k1]
            d = _dot(_bf(_silu(uc) * gate_in), wdn_ref[k0:k1, :])
            val["acc"] = d if "acc" not in val else val["acc"] + d
        return piece

    def c_final():
        g2 = modc_ref[0, 5:6, :]
        y_ref[0] = _rms(x1_sc[...] + g2 * val["acc"]) * nf_ref[...]
        u_ext[0:HIST, :] = u_ext[tile:tile + HIST, :]

    def a_norm():
        sh1, sc1 = moda_ref[0, 0:1, :], moda_ref[0, 1:2, :]
        h_sc[...] = _bf(_rms(xa_ref[0]) * (n1_ref[...] * (1.0 + sc1)) + sh1)

    def a_proj(c0, c1):
        def piece():
            proj_sc[:, c0:c1] = _dot(h_sc[...], win_ref[:, c0:c1])
        return piece

    def b_conv():
        qk_ext[HIST:HIST + tile, :] = proj_sc[:, P_MQ:P_GATE]
        conv = mcb_ref[...] + mcw_ref[M_CONV - 1:M_CONV, :] * qk_ext[HIST:HIST + tile, :]
        for t in range(M_CONV - 1):
            off = HIST - (M_CONV - 1) + t
            conv = conv + mcw_ref[t:t + 1, :] * qk_ext[off:off + tile, :]
        qkc_sc[...] = _silu(conv)
        qk_ext[0:HIST, :] = qk_ext[tile:tile + HIST, :]

    def b_gates(c):
        val["gate", c] = _gate_block(proj_sc[c * CHUNK:(c + 1) * CHUNK, P_GATE:P_HEAD0], gb_ref[...], ops)

    def head_unit(c, hd):
        rows = slice(c * CHUNK, (c + 1) * CHUNK)
        cols = slice(hd * HEAD, (hd + 1) * HEAD)
        base = P_HEAD0 + hd * HEAD_COLS

        def pcol(off):
            return proj_sc[rows, base + off:base + off + HEAD]

        cos_t, sin_t = cos_ref[rows, :], sin_ref[rows, :]
        q = _rope(pcol(O_RQ), cos_t, sin_t)
        k = _rope(pcol(O_RK), cos_t, sin_t) * (HEAD ** -0.5)
        v = pcol(O_RV)
        s_ret = _dot_nt(_bf(q), _bf(k))
        qs = _dot(_bf(q * qdec_ref[hd]), _bf(s_sc[hd]))
        mq = qkc_sc[rows, hd * HEAD:(hd + 1) * HEAD] * (HEAD ** -0.5)
        mk = qkc_sc[rows, GROUP_W + hd * HEAD:GROUP_W + (hd + 1) * HEAD]
        mv = pcol(O_MV)
        s_ml = _dot_nt(_bf(mq), _bf(mk))
        qc = _dot(_bf(mq), _bf(c_sc[hd]))
        out_gate_r = _silu(pcol(O_RG))
        out_gate_m = _sigmoid(pcol(O_MO))
        yield
        o = _dot(_bf(s_ret * inner_ref[hd]), _bf(v)) + qs
        s_upd = _dot_tn(_bf(k * kdec_ref[hd]), _bf(v))
        act, csum, act_t, csum_t = val["gate", c]
        ig_col, ig_row = _pick_lane(act, hd), _pick_row(act_t, hd)
        b_col, b_row = _pick_lane(csum, N_HEADS + hd), _pick_row(csum_t, N_HEADS + hd)
        m_col, n_rows = m_sc[hd][:, 0:1], n_sc[hd]
        logw = jnp.where(ops.causal, b_col - b_row + ig_row, NEG)
        inter = b_col + m_col
        mt = jnp.maximum(inter, jnp.max(logw, axis=-1, keepdims=True))
        sm = s_ml * jnp.exp(logw - mt)
        wi = jnp.exp(inter - mt)
        num = wi * qc + _dot(_bf(sm), _bf(mv))
        den = wi * jnp.sum(mq * n_rows, axis=-1, keepdims=True) + jnp.sum(sm, axis=-1, keepdims=True)
        b_last = ops.last(jnp.broadcast_to(b_col, (CHUNK, HEAD)))
        m_new = ops.last(jnp.broadcast_to(mt, (CHUNK, HEAD)))
        wk = jnp.exp(b_last - b_col + ig_col - m_new)
        wc = jnp.exp(b_last + m_col - m_new)
        kw = mk * wk
        c_upd = _dot_tn(_bf(kw), _bf(mv))
        yield
        s_sc[hd] = cdec_ref[hd] * s_sc[hd] + s_upd
        ymix_sc[rows, cols] = _bf(_head_norm(o, rnw_ref[:, cols]) * out_gate_r)
        c_sc[hd] = wc * c_sc[hd] + c_upd
        n_sc[hd] = wc * n_rows + ops.total(kw)
        m_sc[hd] = m_new
        hm = num / jnp.maximum(jnp.abs(den), jnp.exp(-mt))
        ymix_sc[rows, GROUP_W + hd * HEAD:GROUP_W + (hd + 1) * HEAD] = _bf(
            _head_norm(hm, mnw_ref[:, cols]) * out_gate_m)
        yield

    def ffn_unit(k0, k1):
        h2 = val["h2"]
        u = _dot(h2, wup_ref[:, k0:k1])
        gate_in = _dot(h2, wup_ref[:, D_FF + k0:D_FF + k1])
        yield
        u_ext[HIST:HIST + tile, k0:k1] = u
        uc = fcb_ref[:, k0:k1] + fcw_ref[FF_CONV - 1:FF_CONV, k0:k1] * u
        for t in range(FF_CONV - 1):
            off = HIST - (FF_CONV - 1) + t
            uc = uc + fcw_ref[t:t + 1, k0:k1] * u_ext[off:off + tile, k0:k1]
        d = _dot(_bf(_silu(uc) * gate_in), wdn_ref[k0:k1, :])
        val["acc"] = d if "acc" not in val else val["acc"] + d
        yield

    n_chunks = tile // CHUNK
    heads = [head_unit(c, hd) for c in range(n_chunks) for hd in range(N_HEADS)]
    proj_heads = [a_proj(P_HEAD0 + hd * HEAD_COLS, P_HEAD0 + (hd + 1) * HEAD_COLS) for hd in range(N_HEADS)]
    n_ffn = -(-D_FF // FFN_PIECE)
    ffn = None
    big = []
    for k in range(n_ffn + 1):
        big.append(("ffn", k))
    small = []
    a_norm()
    b_conv()
    for c in range(n_chunks):
        b_gates(c)
    c_out()
    ffn = [ffn_unit(k * FFN_PIECE, min((k + 1) * FFN_PIECE, D_FF)) for k in range(n_ffn)]

    def ffn_slot(k):
        if k < n_ffn:
            next(ffn[k])
        if k >= 1:
            next(ffn[k - 1])

    slots = [lambda k=k: ffn_slot(k) for k in range(n_ffn + 1)]
    slots += [a_proj(0, GROUP_W), a_proj(GROUP_W, P_HEAD0)]
    slot_iter = iter(slots)
    tail = list(proj_heads)
    for u, unit in enumerate(heads):
        next(unit)
        next(slot_iter, lambda: None)()
        next(unit)
        next(slot_iter, lambda: None)()
        next(unit)
        if u >= N_HEADS * (n_chunks - 1):
            pass
    for rest in slot_iter:
        rest()
    c_final()
    for piece in tail:
        piece()

    @pl.when((jb == nt - 1) & (i >= 1) & (i <= n_tiles))
    def _():
        mconv_ref[0] = qk_ext[0:HIST, :]
        ret_ref[0] = s_sc[...]
        mc_ref[0] = c_sc[...]
        for hd in range(N_HEADS):
            mn_ref[0, hd:hd + 1, :] = n_sc[hd][0:1, :]
            mm_ref[0, hd:hd + 1, :] = m_sc[hd][0:1, :]

    @pl.when((jc == nt - 1) & (i >= 2))
    def _():
        ffconv_ref[0] = u_ext[0:HIST, :]


def _const_spec(shape):
    nd = len(shape)
    return pl.BlockSpec(shape, lambda *_: (0,) * nd, pipeline_mode=pl.Buffered(1))


def _prompt_call(x, mod, n1, n2, nf, win, gbias, mcw, mcb, rnw, mnw, wout, wup, fcw, fcb, wdn,
                 cos_t, sin_t, consts):
    nb, seq, _ = x.shape
    tile = PROMPT_TILE
    nt = seq // tile
    n_tiles = nb * nt
    inner, qdec, kdec, cdec = consts
    state = (N_HEADS, HEAD, HEAD)

    def tile_a(i):
        return jnp.minimum(i, n_tiles - 1)

    def tile_b(i):
        return jnp.clip(i - 1, 0, n_tiles - 1)

    def tile_c(i):
        return jnp.clip(i - 2, 0, n_tiles - 1)

    in_specs = [
        pl.BlockSpec((1, tile, D_MODEL), lambda s: (tile_a(s) // nt, tile_a(s) % nt, 0)),
        pl.BlockSpec((1, tile, D_MODEL), lambda s: (tile_c(s) // nt, tile_c(s) % nt, 0)),
        pl.BlockSpec((1, 6, D_MODEL), lambda s: (tile_a(s) // nt, 0, 0)),
        pl.BlockSpec((1, 6, D_MODEL), lambda s: (tile_c(s) // nt, 0, 0)),
        _const_spec((1, D_MODEL)), _const_spec((1, D_MODEL)), _const_spec((1, D_MODEL)),
        _const_spec((D_MODEL, IN_PAD)), _const_spec((1, HEAD)),
        _const_spec((M_CONV, 2 * GROUP_W)), _const_spec((1, 2 * GROUP_W)),
        _const_spec((1, GROUP_W)), _const_spec((1, GROUP_W)),
        _const_spec((2 * GROUP_W, D_MODEL)),
        _const_spec((D_MODEL, 2 * D_FF)),
        _const_spec((FF_CONV, D_FF)), _const_spec((1, D_FF)),
        _const_spec((D_FF, D_MODEL)),
        pl.BlockSpec((tile, HEAD), lambda s: (tile_b(s) % nt, 0)),
        pl.BlockSpec((tile, HEAD), lambda s: (tile_b(s) % nt, 0)),
        _const_spec(state), _const_spec(state), _const_spec(state), _const_spec(state),
    ]
    out_specs = [
        pl.BlockSpec((1, tile, D_MODEL), lambda s: (tile_c(s) // nt, tile_c(s) % nt, 0)),
        pl.BlockSpec((1,) + state, lambda s: (tile_b(s) // nt, 0, 0, 0)),
        pl.BlockSpec((1,) + state, lambda s: (tile_b(s) // nt, 0, 0, 0)),
        pl.BlockSpec((1, N_HEADS, HEAD), lambda s: (tile_b(s) // nt, 0, 0)),
        pl.BlockSpec((1, N_HEADS, HEAD), lambda s: (tile_b(s) // nt, 0, 0)),
        pl.BlockSpec((1, HIST, 2 * GROUP_W), lambda s: (tile_b(s) // nt, 0, 0)),
        pl.BlockSpec((1, HIST, D_FF), lambda s: (tile_c(s) // nt, 0, 0)),
    ]
    out_shape = [
        jax.ShapeDtypeStruct((nb, seq, D_MODEL), F32),
        jax.ShapeDtypeStruct((nb,) + state, F32),
        jax.ShapeDtypeStruct((nb,) + state, F32),
        jax.ShapeDtypeStruct((nb, N_HEADS, HEAD), F32),
        jax.ShapeDtypeStruct((nb, N_HEADS, HEAD), F32),
        jax.ShapeDtypeStruct((nb, HIST, 2 * GROUP_W), F32),
        jax.ShapeDtypeStruct((nb, HIST, D_FF), F32),
    ]
    scratch = [
        pltpu.VMEM((tile, IN_PAD), F32),
        pltpu.VMEM((tile + HIST, 2 * GROUP_W), F32),
        pltpu.VMEM((tile, 2 * GROUP_W), F32),
        pltpu.VMEM((tile, D_MODEL), BF16),
        pltpu.VMEM((tile + HIST, D_FF), F32),
        pltpu.VMEM((tile, 2 * GROUP_W), BF16),
        pltpu.VMEM((tile, D_MODEL), F32),
        pltpu.VMEM(state, F32), pltpu.VMEM(state, F32), pltpu.VMEM(state, F32), pltpu.VMEM(state, F32),
    ]
    return pl.pallas_call(
        functools.partial(_prompt_kernel, tile=tile, nt=nt),
        grid=(n_tiles + 2,),
        in_specs=in_specs, out_specs=out_specs, out_shape=out_shape, scratch_shapes=scratch,
        compiler_params=pltpu.CompilerParams(
            dimension_semantics=("arbitrary",),
            vmem_limit_bytes=V7X_VMEM_BYTES - 4 * 1024 * 1024),
        name="prompt_layer",
    )(x, x, mod, mod, n1, n2, nf, win, gbias, mcw, mcb, rnw, mnw, wout, wup, fcw, fcb, wdn,
      cos_t, sin_t, inner, qdec, kdec, cdec)


def _s_in_kernel(x_ref, sh1_ref, sc1_ref, n1_ref, win_ref, o_ref, *, seq):
    a1 = _repeat_rows(n1_ref[...] * (1.0 + sc1_ref[...]), seq)
    h = _bf(_rms(x_ref[...]) * a1 + _repeat_rows(sh1_ref[...], seq))
    o_ref[...] = _dot(h, win_ref[...])


def _s_in_call(xs, mod_s, n1, win, seq):
    rows = xs.shape[0]
    nbatch = rows // seq
    nblk = 3
    wblk = IN_PAD // nblk
    return pl.pallas_call(
        functools.partial(_s_in_kernel, seq=seq),
        grid=(nblk,),
        in_specs=[pl.BlockSpec((rows, D_MODEL), lambda n: (0, 0)),
                  pl.BlockSpec((nbatch, D_MODEL), lambda n: (0, 0)),
                  pl.BlockSpec((nbatch, D_MODEL), lambda n: (0, 1)),
                  pl.BlockSpec((1, D_MODEL), lambda n: (0, 0)),
                  pl.BlockSpec((D_MODEL, wblk), lambda n: (0, n))],
        out_specs=pl.BlockSpec((rows, wblk), lambda n: (0, n)),
        out_shape=jax.ShapeDtypeStruct((rows, IN_PAD), F32),
        name="sample_in",
    )(xs, mod_s, mod_s, n1, win)


def _s_mix_kernel(rq_ref, rk_ref, rv_ref, rg_ref, mq_ref, mk_ref, mv_ref, mo_ref, gate_ref,
                  hq_ref, hk_ref, s0_ref, c0_ref, n0_ref, m0_ref, gb_ref,
                  mcwq_ref, mcwk_ref, mcbq_ref, mcbk_ref, rnw_ref, mnw_ref,
                  cos_ref, sin_ref, inner_ref, qdec_ref, kdec_ref, cdec_ref,
                  yr_ref, ym_ref, s1_ref, c1_ref, n1_ref, m1_ref, o_sc, *, seq):
    hd = pl.program_id(1)
    rows = rq_ref.shape[0]
    nbatch = rows // seq
    ops = _ChunkOps(rows, seq)
    row_t = lax.broadcasted_iota(jnp.int32, (rows, HEAD), 0) & (seq - 1)
    tile_half = lax.broadcasted_iota(jnp.int32, (2 * seq, HEAD), 0) >= seq
    lane_b = lax.broadcasted_iota(jnp.int32, (HEAD, rows), 1) >> (seq.bit_length() - 1)

    def state_read(state_ref):
        def read(qq):
            for i in range(nbatch // 2):
                qt = qq[2 * seq * i:2 * seq * (i + 1), :]
                lo = _dot(_bf(jnp.where(tile_half, 0.0, qt)), _bf(state_ref[2 * i, 0]))
                hi = _dot(_bf(jnp.where(tile_half, qt, 0.0)), _bf(state_ref[2 * i + 1, 0]))
                o_sc[2 * seq * i:2 * seq * (i + 1), :] = lo + hi
            return o_sc[...]
        return read

    def state_write(new_ref, old_ref, decay_rows, kx, v):
        kx_t = kx.T
        vb = _bf(v)
        for b in range(nbatch):
            upd = _dot(_bf(jnp.where(lane_b == b, kx_t, 0.0)), vb)
            dec = jnp.broadcast_to(decay_rows[seq * b:seq * b + 1, :], (HEAD, HEAD))
            new_ref[b, 0] = dec * old_ref[b, 0] + upd

    cos_t, sin_t = cos_ref[...], sin_ref[...]
    q = _rope(rq_ref[...], cos_t, sin_t)
    k = _rope(rk_ref[...], cos_t, sin_t) * (HEAD ** -0.5)
    v = rv_ref[...]
    o, kd = _ret_block(q, k, v, inner_ref[0], qdec_ref[0], kdec_ref[0], state_read(s0_ref))
    state_write(s1_ref, s0_ref, cdec_ref[0], kd, v)
    yr_ref[...] = _bf(_head_norm(o, rnw_ref[...]) * _silu(rg_ref[...]))

    def conv(x, hist, w_ref, b_ref):
        out = b_ref[...] + w_ref[M_CONV - 1:M_CONV, :] * x
        for d in range(1, M_CONV):
            prev = jnp.where(row_t >= d, _roll_rows(x, d), _roll_rows(hist, d - (M_CONV - 1)))
            out = out + w_ref[M_CONV - 1 - d:M_CONV - d, :] * prev
        return out

    mq = _silu(conv(mq_ref[...], hq_ref[...], mcwq_ref, mcbq_ref)) * (HEAD ** -0.5)
    mk = _silu(conv(mk_ref[...], hk_ref[...], mcwk_ref, mcbk_ref))
    mv = mv_ref[...]
    act, csum, act_t, csum_t = _gate_block(gate_ref[...], gb_ref[...], ops)
    m_col = m0_ref[0]
    hm, kw, wc, m_new, n_new = _mlstm_block(
        mq, mk, mv,
        _pick_lane(act, hd), _pick_row(act_t, hd),
        _pick_lane(csum, N_HEADS + hd), _pick_row(csum_t, N_HEADS + hd),
        m_col, n0_ref[...], ops, state_read(c0_ref))
    state_write(c1_ref, c0_ref, wc, kw, mv)
    n1_ref[...] = n_new
    m1_ref[...] = m_new
    ym_ref[...] = _bf(_head_norm(hm, mnw_ref[...]) * _sigmoid(mo_ref[...]))


def _s_mix_call(proj, hist, s0, c0, n_rows, m_rows, gbias, mcw, mcb, rnw, mnw, cos_t, sin_t, consts, seq):
    rows_all = proj.shape[0]
    blk = SAMPLE_GROUP * seq
    ng = rows_all // blk
    inner, qdec, kdec, cdec = consts

    def col(cb):
        return pl.BlockSpec((blk, HEAD), lambda g, h, cb=cb: (g, cb + h))

    def hcol(off):
        return pl.BlockSpec(
            (blk, HEAD), lambda g, h, off=off: (g, (P_HEAD0 + off) // HEAD + h * (HEAD_COLS // HEAD)))

    def head_const(arr_rows):
        return pl.BlockSpec((arr_rows, HEAD), lambda g, h: (0, h))

    st_spec = pl.BlockSpec((SAMPLE_GROUP, 1, HEAD, HEAD), lambda g, h: (g, h, 0, 0))
    hconst = pl.BlockSpec((1, HEAD, HEAD), lambda g, h: (h, 0, 0))
    in_specs = [
        hcol(O_RQ), hcol(O_RK), hcol(O_RV), hcol(O_RG),
        col(P_MQ // HEAD), col(P_MK // HEAD), hcol(O_MV), hcol(O_MO),
        pl.BlockSpec((blk, HEAD), lambda g, h: (g, P_GATE // HEAD)),
        col(0), col(N_HEADS),
        st_spec, st_spec,
        pl.BlockSpec((blk, HEAD), lambda g, h: (g, h)),
        pl.BlockSpec((1, blk, 1), lambda g, h: (h, g, 0)),
        pl.BlockSpec((1, HEAD), lambda g, h: (0, 0)),
        head_const(M_CONV), pl.BlockSpec((M_CONV, HEAD), lambda g, h: (0, N_HEADS + h)),
        head_const(1), pl.BlockSpec((1, HEAD), lambda g, h: (0, N_HEADS + h)),
        head_const(1), head_const(1),
        pl.BlockSpec((blk, HEAD), lambda g, h: (0, 0)),
        pl.BlockSpec((blk, HEAD), lambda g, h: (0, 0)),
        hconst, hconst, hconst, hconst,
    ]
    out_specs = [
        pl.BlockSpec((blk, HEAD), lambda g, h: (g, h)),
        pl.BlockSpec((blk, HEAD), lambda g, h: (g, h)),
        st_spec, st_spec,
        pl.BlockSpec((blk, HEAD), lambda g, h: (g, h)),
        pl.BlockSpec((blk, HEAD), lambda g, h: (g, h)),
    ]
    out_shape = [
        jax.ShapeDtypeStruct((rows_all, GROUP_W), BF16),
        jax.ShapeDtypeStruct((rows_all, GROUP_W), BF16),
        jax.ShapeDtypeStruct(s0.shape, F32),
        jax.ShapeDtypeStruct(c0.shape, F32),
        jax.ShapeDtypeStruct((rows_all, GROUP_W), F32),
        jax.ShapeDtypeStruct((rows_all, GROUP_W), F32),
    ]
    return pl.pallas_call(
        functools.partial(_s_mix_kernel, seq=seq),
        grid=(ng, N_HEADS),
        in_specs=in_specs, out_specs=out_specs, out_shape=out_shape,
        scratch_shapes=[pltpu.VMEM((blk, HEAD), F32)],
        compiler_params=pltpu.CompilerParams(dimension_semantics=("arbitrary", "arbitrary")),
        name="sample_mix",
    )(proj, proj, proj, proj, proj, proj, proj, proj, proj, hist, hist, s0, c0, n_rows, m_rows, gbias,
      mcw, mcw, mcb, mcb, rnw, mnw, cos_t, sin_t, inner, qdec, kdec, cdec)


def _s_out_kernel(x_ref, yr_ref, ym_ref, g1_ref, sh2_ref, sc2_ref, g2_ref, n2_ref, nf_ref,
                  wout_ref, wup_ref, fcw_ref, fcb_ref, wdn_ref, fh_ref,
                  y_ref, u_ref, *, seq):
    rows = x_ref.shape[0]
    row_t = lax.broadcasted_iota(jnp.int32, (rows, D_FF), 0) & (seq - 1)
    mix = _dot(yr_ref[...], wout_ref[0:GROUP_W, :]) + _dot(ym_ref[...], wout_ref[GROUP_W:2 * GROUP_W, :])
    x1 = x_ref[...] + _repeat_rows(g1_ref[...], seq) * mix

    def conv_fn(u):
        hist = fh_ref[...]
        uc = fcb_ref[...] + fcw_ref[FF_CONV - 1:FF_CONV, :] * u
        for d in range(1, FF_CONV):
            prev = jnp.where(row_t >= d, _roll_rows(u, d), _roll_rows(hist, d - (FF_CONV - 1)))
            uc = uc + fcw_ref[FF_CONV - 1 - d:FF_CONV - d, :] * prev
        return uc

    y, u = _ffn(x1, _repeat_rows(n2_ref[...] * (1.0 + sc2_ref[...]), seq), _repeat_rows(sh2_ref[...], seq),
                _repeat_rows(g2_ref[...], seq), nf_ref[...], wup_ref, wdn_ref, conv_fn)
    y_ref[...] = y
    u_ref[...] = u


def _s_out_call(xs, yr, ym, mod_s, n2, nf, wout, wup, fcw, fcb, wdn, fhist, seq):
    rows_all = xs.shape[0]
    blk = S_OUT_ROWS
    nblk = rows_all // blk

    def modcol(cb):
        return pl.BlockSpec((blk // seq, D_MODEL), lambda i, cb=cb: (i, cb))

    in_specs = [
        pl.BlockSpec((blk, D_MODEL), lambda i: (i, 0)),
        pl.BlockSpec((blk, GROUP_W), lambda i: (i, 0)),
        pl.BlockSpec((blk, GROUP_W), lambda i: (i, 0)),
        modcol(2), modcol(3), modcol(4), modcol(5),
        _const_spec((1, D_MODEL)), _const_spec((1, D_MODEL)),
        _const_spec((2 * GROUP_W, D_MODEL)),
        _const_spec((D_MODEL, 2 * D_FF)),
        _const_spec((FF_CONV, D_FF)), _const_spec((1, D_FF)),
        _const_spec((D_FF, D_MODEL)),
        pl.BlockSpec((blk, D_FF), lambda i: (i, 0)),
    ]
    return pl.pallas_call(
        functools.partial(_s_out_kernel, seq=seq),
        grid=(nblk,),
        in_specs=in_specs,
        out_specs=[pl.BlockSpec((blk, D_MODEL), lambda i: (i, 0)),
                   pl.BlockSpec((blk, D_FF), lambda i: (i, 0))],
        out_shape=[jax.ShapeDtypeStruct((rows_all, D_MODEL), F32),
                   jax.ShapeDtypeStruct((rows_all, D_FF), F32)],
        compiler_params=pltpu.CompilerParams(
            dimension_semantics=("arbitrary",),
            vmem_limit_bytes=V7X_VMEM_BYTES - 4 * 1024 * 1024),
        name="sample_out",
    )(xs, yr, ym, mod_s, mod_s, mod_s, mod_s, n2, nf, wout, wup, fcw, fcb, wdn, fhist)


def _rope_tables(pos):
    half = HEAD // 2
    inv = ROPE_THETA ** (-np.arange(half, dtype=np.float64) / half)
    ang = np.asarray(pos, np.float64)[:, None] * inv[None, :]
    cos, sin = np.cos(ang), np.sin(ang)
    return (jnp.asarray(np.concatenate([cos, cos], axis=-1), F32),
            jnp.asarray(np.concatenate([-sin, sin], axis=-1), F32))


def _retention_consts(cc, rows):
    lg = np.log1p(-np.exp2(-5.0 - np.arange(N_HEADS, dtype=np.float64)))
    ridx = np.arange(rows)
    t = (ridx % cc).astype(np.float64)
    rel = t[:, None] - t[None, :]
    same = (ridx[:, None] // cc) == (ridx[None, :] // cc)
    inner = np.where(same & (rel >= 0), np.exp(np.maximum(rel, 0.0) * lg[:, None, None]), 0.0)
    qdec = np.exp((t + 1.0) * lg[:, None])
    kdec = np.exp((cc - 1.0 - t) * lg[:, None])
    cdec = np.exp(cc * lg)
    wide = (N_HEADS, rows, HEAD)
    return tuple(jnp.asarray(a, F32) for a in (
        inner,
        np.broadcast_to(qdec[:, :, None], wide),
        np.broadcast_to(kdec[:, :, None], wide),
        np.broadcast_to(cdec[:, None, None], wide)))


def _regroup_w_in(w):
    wb = _bf(w)
    gw = GROUP_W
    d = wb.shape[0]
    groups = wb[:, :8 * gw].reshape(d, 8, N_HEADS, HEAD)
    per_head = jnp.concatenate([groups[:, 0:4], groups[:, 6:8]], axis=1)
    per_head = per_head.transpose(0, 2, 1, 3).reshape(d, N_HEADS * HEAD_COLS)
    gates = jnp.pad(wb[:, 8 * gw:], ((0, 0), (0, HEAD - 2 * N_HEADS)))
    return jnp.concatenate([wb[:, 4 * gw:6 * gw], gates, per_head], axis=1)


def kernel(x_prompt, x_sample, c_prompt, c_sample, state_ret, state_mlstm_C, state_mlstm_n, state_mlstm_m, state_mconv, state_ffconv, w_ada, b_ada, norm1_w, norm2_w, w_in, b_igate, b_fgate, mconv_w, mconv_b, ret_norm_w, mlstm_norm_w, w_out, w_up, ffconv_w, ffconv_b, w_down, final_w):
    assert w_ada.shape[0] == 1, "single-layer kernel"
    nb, seq_p, _ = x_prompt.shape
    ns, seq_s, _ = x_sample.shape
    assert seq_p % PROMPT_TILE == 0 and seq_s == 4 and ns % SAMPLE_GROUP == 0
    rows_s = ns * seq_s

    win = _regroup_w_in(w_in[0])
    wout = _bf(w_out[0])
    wup = _bf(w_up[0])
    wdn = _bf(w_down[0])
    gbias = jnp.pad(jnp.concatenate([b_igate[0], b_fgate[0]]), (0, HEAD - 2 * N_HEADS))[None, :]
    n1, n2, nf = norm1_w, norm2_w, final_w[None, :]
    mcw, mcb = mconv_w[0], mconv_b
    rnw, mnw = ret_norm_w, mlstm_norm_w
    fcw, fcb = ffconv_w[0], ffconv_b

    mod = _ada_call(jnp.concatenate([c_prompt, c_sample], axis=0), w_ada[0], b_ada)
    mod_p = mod[:nb].reshape(nb, 6, D_MODEL)
    mod_s = mod[nb:]

    cos_p, sin_p = _rope_tables(np.arange(seq_p))
    y_p, ret_p, mc_p, mn_p, mm_p, mconv_t, ffconv_t = _prompt_call(
        x_prompt, mod_p, n1, n2, nf, win, gbias, mcw, mcb, rnw, mnw, wout, wup, fcw, fcb, wdn,
        cos_p, sin_p, _retention_consts(CHUNK, CHUNK))
    mconv_p = mconv_t[:, HIST - (M_CONV - 1):, :]
    ffconv_p = ffconv_t[:, HIST - (FF_CONV - 1):, :]

    xs = x_sample.reshape(rows_s, D_MODEL)
    proj_s = _s_in_call(xs, mod_s, n1, win, seq_s)
    blk = SAMPLE_GROUP * seq_s
    cos_s, sin_s = _rope_tables(PAST_LEN + (np.arange(blk) % seq_s))
    hist_m = jnp.pad(state_mconv[0], ((0, 0), (0, seq_s - (M_CONV - 1)), (0, 0))).reshape(rows_s, 2 * GROUP_W)
    n_rows = jnp.repeat(state_mlstm_n[0].reshape(ns, GROUP_W), seq_s, axis=0)
    m_rows = jnp.repeat(state_mlstm_m[0], seq_s, axis=0).T[:, :, None]
    yr, ym, ret_s, mc_s, n_new, m_new = _s_mix_call(
        proj_s, hist_m, state_ret[0], state_mlstm_C[0], n_rows, m_rows, gbias, mcw, mcb, rnw, mnw,
        cos_s, sin_s, _retention_consts(seq_s, blk), seq_s)
    hist_f = jnp.pad(state_ffconv[0], ((0, 0), (0, seq_s - (FF_CONV - 1)), (0, 0))).reshape(rows_s, D_FF)
    y_s, u_s = _s_out_call(xs, yr, ym, mod_s, n2, nf, wout, wup, fcw, fcb, wdn, hist_f, seq_s)

    mn_s = n_new[::seq_s].reshape(ns, N_HEADS, HEAD)
    mm_s = m_new[::seq_s].reshape(ns, N_HEADS, HEAD)[:, :, 0]
    mconv_s = jnp.stack([proj_s[t::seq_s, P_MQ:P_GATE] for t in range(seq_s - (M_CONV - 1), seq_s)], axis=1)
    ffconv_s = jnp.stack([u_s[t::seq_s, :] for t in range(seq_s - (FF_CONV - 1), seq_s)], axis=1)

    return (y_p, y_s.reshape(ns, seq_s, D_MODEL),
            ret_p[None], mc_p[None], mn_p[None], mm_p[:, :, 0][None], mconv_p[None], ffconv_p[None],
            ret_s[None], mc_s[None], mn_s[None], mm_s[None], mconv_s[None], ffconv_s[None])
```

```python
import functools

import jax
import jax.numpy as jnp
import numpy as np
from jax import lax
from jax.experimental import pallas as pl
from jax.experimental.pallas import tpu as pltpu

F32 = jnp.float32
BF16 = jnp.bfloat16

D_MODEL = 1024
N_HEADS = 4
HEAD = 128
GROUP_W = N_HEADS * HEAD
D_FF = 2816
M_CONV = 4
FF_CONV = 3
CHUNK = 128
PAST_LEN = 16384
ROPE_THETA = 10000.0
RMS_EPS = 1e-6
GN_EPS = 1e-5
NEG = -1e30

P_MQ, P_MK, P_GATE = 0, GROUP_W, 2 * GROUP_W
P_HEAD0 = 2 * GROUP_W + HEAD
HEAD_COLS = 6 * HEAD
O_RQ, O_RK, O_RV, O_RG, O_MV, O_MO = (i * HEAD for i in range(6))
IN_PAD = P_HEAD0 + N_HEADS * HEAD_COLS
FFN_PIECE = 256
HIST = 8

V7X_VMEM_BYTES = 64 * 1024 * 1024
PROMPT_TILE = 256
SAMPLE_GROUP = 32
S_OUT_ROWS = 256


def _dot(a, b):
    return jnp.dot(a, b, preferred_element_type=F32)


def _dot_nt(a, b):
    return lax.dot_general(a, b, (((1,), (1,)), ((), ())), preferred_element_type=F32)


def _dot_tn(a, b):
    return lax.dot_general(a, b, (((0,), (0,)), ((), ())), preferred_element_type=F32)


def _bf(x):
    return x.astype(BF16)


def _silu(x):
    return x * (1.0 / (1.0 + jnp.exp(-x)))


def _sigmoid(x):
    return 1.0 / (1.0 + jnp.exp(-x))


def _log_sigmoid(x):
    return jnp.minimum(x, 0.0) - jnp.log(1.0 + jnp.exp(-jnp.abs(x)))


def _rms(x):
    return x * lax.rsqrt(jnp.mean(x * x, axis=-1, keepdims=True) + RMS_EPS)


def _head_norm(x, w_row):
    mu = jnp.mean(x, axis=-1, keepdims=True)
    xc = x - mu
    var = jnp.mean(xc * xc, axis=-1, keepdims=True)
    return xc * lax.rsqrt(var + GN_EPS) * w_row


def _rope(x, cos_t, sin_t):
    return x * cos_t + pltpu.roll(x, HEAD // 2, 1) * sin_t


def _pick_lane(x, idx):
    lane = lax.broadcasted_iota(jnp.int32, x.shape, 1)
    return jnp.sum(jnp.where(lane == idx, x, 0.0), axis=1, keepdims=True)


def _pick_row(x, idx):
    row = lax.broadcasted_iota(jnp.int32, x.shape, 0)
    return jnp.sum(jnp.where(row == idx, x, 0.0), axis=0, keepdims=True)


def _split3(x):
    hi = _bf(x)
    r = x - hi.astype(F32)
    mid = _bf(r)
    lo = _bf(r - mid.astype(F32))
    return hi, mid, lo


def _roll_rows(x, shift):
    shift = shift % x.shape[0]
    return x if shift == 0 else pltpu.roll(x, shift, 0)


class _ChunkOps:
    def __init__(self, rows, seq):
        self.rows, self.seq = rows, seq
        r = lax.broadcasted_iota(jnp.int32, (rows, rows), 0)
        c = lax.broadcasted_iota(jnp.int32, (rows, rows), 1)
        if seq == rows:
            self.causal = c <= r
            self.tril = jnp.where(self.causal, 1.0, 0.0).astype(BF16)
        else:
            shift = seq.bit_length() - 1
            same = (r >> shift) == (c >> shift)
            self.causal = same & (c <= r)
            self.t = lax.broadcasted_iota(jnp.int32, (rows, HEAD), 0) & (seq - 1)

    def cumsum(self, x):
        if self.seq == self.rows:
            hi, mid, lo = _split3(x)
            return _dot(self.tril, hi) + _dot(self.tril, mid) + _dot(self.tril, lo)
        out = x
        for d in range(1, self.seq):
            out = out + jnp.where(self.t >= d, _roll_rows(x, d), 0.0)
        return out

    def last(self, x):
        if self.seq == self.rows:
            return jnp.broadcast_to(x[self.rows - 1:self.rows, :], x.shape)
        y = jnp.where(self.t == self.seq - 1, x, 0.0)
        out = y
        for d in range(1, self.seq):
            out = out + _roll_rows(y, -d)
        return out

    def total(self, x):
        if self.seq == self.rows:
            return jnp.broadcast_to(jnp.sum(x, axis=0, keepdims=True), x.shape)
        assert self.seq == 4
        odd = (self.t & 1) == 1
        p = x + jnp.where(odd, _roll_rows(x, 1), _roll_rows(x, -1))
        return p + jnp.where(self.t >= 2, _roll_rows(p, 2), _roll_rows(p, -2))


def _repeat_rows(x, seq):
    n = x.shape[0]
    r = lax.broadcasted_iota(jnp.int32, (n * seq, n), 0)
    c = lax.broadcasted_iota(jnp.int32, (n * seq, n), 1)
    sel = jnp.where((r >> (seq.bit_length() - 1)) == c, 1.0, 0.0).astype(BF16)
    hi, mid, lo = _split3(x)
    return _dot(sel, hi) + _dot(sel, mid) + _dot(sel, lo)


def _gate_block(gates, gbias, ops):
    z = gates + gbias
    lane = lax.broadcasted_iota(jnp.int32, z.shape, 1)
    act = jnp.where(lane < N_HEADS, z, _log_sigmoid(z))
    csum = ops.cumsum(act)
    return act, csum, act.T, csum.T


def _ret_block(q, k, v, inner, qdec, kdec, state_read):
    s = _dot_nt(_bf(q), _bf(k)) * inner
    o = _dot(_bf(s), _bf(v)) + state_read(q * qdec)
    return o, k * kdec


def _mlstm_block(q, k, v, ig_col, ig_row, b_col, b_row, m_col, n_rows, ops, state_read):
    rows = q.shape[0]
    logw = jnp.where(ops.causal, b_col - b_row + ig_row, NEG)
    inter = b_col + m_col
    mt = jnp.maximum(inter, jnp.max(logw, axis=-1, keepdims=True))
    s = _dot_nt(_bf(q), _bf(k)) * jnp.exp(logw - mt)
    wi = jnp.exp(inter - mt)
    num = wi * state_read(q) + _dot(_bf(s), _bf(v))
    den = wi * jnp.sum(q * n_rows, axis=-1, keepdims=True) + jnp.sum(s, axis=-1, keepdims=True)
    h = num / jnp.maximum(jnp.abs(den), jnp.exp(-mt))
    b_last = ops.last(jnp.broadcast_to(b_col, (rows, HEAD)))
    m_new = ops.last(jnp.broadcast_to(mt, (rows, HEAD)))
    wk = jnp.exp(b_last - b_col + ig_col - m_new)
    wc = jnp.exp(b_last + m_col - m_new)
    kw = k * wk
    n_new = wc * n_rows + ops.total(kw)
    return h, kw, wc, m_new, n_new


def _ada_kernel(c_ref, w_ref, b_ref, o_ref):
    c = c_ref[...]
    o_ref[...] = _dot(_bf(_silu(c)), _bf(w_ref[...])) + b_ref[...]


def _ada_call(c_all, w_ada, b_ada):
    rows = c_all.shape[0]
    n_blk = w_ada.shape[1] // D_MODEL
    return pl.pallas_call(
        _ada_kernel,
        grid=(n_blk,),
        in_specs=[pl.BlockSpec((rows, D_MODEL), lambda n: (0, 0)),
                  pl.BlockSpec((D_MODEL, D_MODEL), lambda n: (0, n)),
                  pl.BlockSpec((1, D_MODEL), lambda n: (0, n))],
        out_specs=pl.BlockSpec((rows, D_MODEL), lambda n: (0, n)),
        out_shape=jax.ShapeDtypeStruct((rows, w_ada.shape[1]), F32),
        name="ada",
    )(c_all, w_ada, b_ada)


def _ffn(x1, a2, sh2, g2, nf, wup_ref, wdn_ref, conv_fn):
    h2 = _bf(_rms(x1) * a2 + sh2)
    u = _dot(h2, wup_ref[:, 0:D_FF])
    val = _dot(h2, wup_ref[:, D_FF:2 * D_FF])
    uc = conv_fn(u)
    act = _bf(_silu(uc) * val)
    x2 = x1 + g2 * _dot(act, wdn_ref[...])
    return _rms(x2) * nf, u


def _prompt_kernel(xa_ref, xc_ref, moda_ref, modc_ref, n1_ref, n2_ref, nf_ref, win_ref, gb_ref,
                   mcw_ref, mcb_ref, rnw_ref, mnw_ref, wout_ref, wup_ref, fcw_ref, fcb_ref, wdn_ref,
                   cos_ref, sin_ref, inner_ref, qdec_ref, kdec_ref, cdec_ref,
                   y_ref, ret_ref, mc_ref, mn_ref, mm_ref, mconv_ref, ffconv_ref,
                   proj_sc, qk_ext, qkc_sc, h_sc, u_ext, ymix_sc, x1_sc, s_sc, c_sc, n_sc, m_sc, *, tile, nt):
    i = pl.program_id(0)
    n_tiles = pl.num_programs(0) - 2
    jb = lax.rem(jnp.clip(i - 1, 0, n_tiles - 1), nt)
    jc = lax.rem(jnp.clip(i - 2, 0, n_tiles - 1), nt)

    @pl.when(i == 0)
    def _():
        proj_sc[...] = jnp.zeros_like(proj_sc)
        ymix_sc[...] = jnp.zeros_like(ymix_sc)

    @pl.when(jb == 0)
    def _():
        qk_ext[0:HIST, :] = jnp.zeros((HIST, 2 * GROUP_W), F32)
        s_sc[...] = jnp.zeros_like(s_sc)
        c_sc[...] = jnp.zeros_like(c_sc)
        n_sc[...] = jnp.zeros_like(n_sc)
        m_sc[...] = jnp.zeros_like(m_sc)

    @pl.when(jc == 0)
    def _():
        u_ext[0:HIST, :] = jnp.zeros((HIST, D_FF), F32)

    val = {}
    ops = _ChunkOps(CHUNK, CHUNK)

    def c_out():
        g1 = modc_ref[0, 2:3, :]
        sh2, sc2 = modc_ref[0, 3:4, :], modc_ref[0, 4:5, :]
        x1 = xc_ref[0] + g1 * _dot(ymix_sc[...], wout_ref[...])
        x1_sc[...] = x1
        val["h2"] = _bf(_rms(x1) * (n2_ref[...] * (1.0 + sc2)) + sh2)

    def c_ffn(k0, k1):
        def piece():
            h2 = val["h2"]
            u = _dot(h2, wup_ref[:, k0:k1])
            gate_in = _dot(h2, wup_ref[:, D_FF + k0:D_FF + k1])
            u_ext[HIST:HIST + tile, k0:k1] = u
            uc = fcb_ref[:, k0:k1] + fcw_ref[FF_CONV - 1:FF_CONV, k0:k1] * u
            for t in range(FF_CONV - 1):
                off = HIST - (FF_CONV - 1) + t
                uc = uc + fcw_ref[t:t + 1, k0:k1] * u_ext[off:off + tile, k0:k1]
            d = _dot(_bf(_silu(uc) * gate_in), wdn_ref[k0:k1, :])
            val["acc"] = d if "acc" not in val else val["acc"] + d
        return piece

    def c_final():
        g2 = modc_ref[0, 5:6, :]
        y_ref[0] = _rms(x1_sc[...] + g2 * val["acc"]) * nf_ref[...]
        u_ext[0:HIST, :] = u_ext[tile:tile + HIST, :]

    def a_norm():
        sh1, sc1 = moda_ref[0, 0:1, :], moda_ref[0, 1:2, :]
        h_sc[...] = _bf(_rms(xa_ref[0]) * (n1_ref[...] * (1.0 + sc1)) + sh1)

    def a_proj(c0, c1):
        def piece():
            proj_sc[:, c0:c1] = _dot(h_sc[...], win_ref[:, c0:c1])
        return piece

    def b_conv():
        qk_ext[HIST:HIST + tile, :] = proj_sc[:, P_MQ:P_GATE]
        conv = mcb_ref[...] + mcw_ref[M_CONV - 1:M_CONV, :] * qk_ext[HIST:HIST + tile, :]
        for t in range(M_CONV - 1):
            off = HIST - (M_CONV - 1) + t
            conv = conv + mcw_ref[t:t + 1, :] * qk_ext[off:off + tile, :]
        qkc_sc[...] = _silu(conv)
        qk_ext[0:HIST, :] = qk_ext[tile:tile + HIST, :]

    def b_gates(c):
        val["gate", c] = _gate_block(proj_sc[c * CHUNK:(c + 1) * CHUNK, P_GATE:P_HEAD0], gb_ref[...], ops)

    def head_unit(c, hd):
        rows = slice(c * CHUNK, (c + 1) * CHUNK)
        cols = slice(hd * HEAD, (hd + 1) * HEAD)
        base = P_HEAD0 + hd * HEAD_COLS

        def pcol(off):
            return proj_sc[rows, base + off:base + off + HEAD]

        cos_t, sin_t = cos_ref[rows, :], sin_ref[rows, :]
        q = _rope(pcol(O_RQ), cos_t, sin_t)
        k = _rope(pcol(O_RK), cos_t, sin_t) * (HEAD ** -0.5)
        v = pcol(O_RV)
        s_ret = _dot_nt(_bf(q), _bf(k))
        qs = _dot(_bf(q * qdec_ref[hd]), _bf(s_sc[hd]))
        mq = qkc_sc[rows, hd * HEAD:(hd + 1) * HEAD] * (HEAD ** -0.5)
        mk = qkc_sc[rows, GROUP_W + hd * HEAD:GROUP_W + (hd + 1) * HEAD]
        mv = pcol(O_MV)
        s_ml = _dot_nt(_bf(mq), _bf(mk))
        qc = _dot(_bf(mq), _bf(c_sc[hd]))
        out_gate_r = _silu(pcol(O_RG))
        out_gate_m = _sigmoid(pcol(O_MO))
        yield
        o = _dot(_bf(s_ret * inner_ref[hd]), _bf(v)) + qs
        s_upd = _dot_tn(_bf(k * kdec_ref[hd]), _bf(v))
        act, csum, act_t, csum_t = val["gate", c]
        ig_col, ig_row = _pick_lane(act, hd), _pick_row(act_t, hd)
        b_col, b_row = _pick_lane(csum, N_HEADS + hd), _pick_row(csum_t, N_HEADS + hd)
        m_col, n_rows = m_sc[hd][:, 0:1], n_sc[hd]
        logw = jnp.where(ops.causal, b_col - b_row + ig_row, NEG)
        inter = b_col + m_col
        mt = jnp.maximum(inter, jnp.max(logw, axis=-1, keepdims=True))
        sm = s_ml * jnp.exp(logw - mt)
        wi = jnp.exp(inter - mt)
        num = wi * qc + _dot(_bf(sm), _bf(mv))
        den = wi * jnp.sum(mq * n_rows, axis=-1, keepdims=True) + jnp.sum(sm, axis=-1, keepdims=True)
        b_last = ops.last(jnp.broadcast_to(b_col, (CHUNK, HEAD)))
        m_new = ops.last(jnp.broadcast_to(mt, (CHUNK, HEAD)))
        wk = jnp.exp(b_last - b_col + ig_col - m_new)
        wc = jnp.exp(b_last + m_col - m_new)
        kw = mk * wk
        c_upd = _dot_tn(_bf(kw), _bf(mv))
        yield
        s_sc[hd] = cdec_ref[hd] * s_sc[hd] + s_upd
        ymix_sc[rows, cols] = _bf(_head_norm(o, rnw_ref[:, cols]) * out_gate_r)
        c_sc[hd] = wc * c_sc[hd] + c_upd
        n_sc[hd] = wc * n_rows + ops.total(kw)
        m_sc[hd] = m_new
        hm = num / jnp.maximum(jnp.abs(den), jnp.exp(-mt))
        ymix_sc[rows, GROUP_W + hd * HEAD:GROUP_W + (hd + 1) * HEAD] = _bf(
            _head_norm(hm, mnw_ref[:, cols]) * out_gate_m)
        yield

    def ffn_unit(k0, k1):
        h2 = val["h2"]
        u = _dot(h2, wup_ref[:, k0:k1])
        gate_in = _dot(h2, wup_ref[:, D_FF + k0:D_FF + k1])
        yield
        u_ext[HIST:HIST + tile, k0:k1] = u
        uc = fcb_ref[:, k0:k1] + fcw_ref[FF_CONV - 1:FF_CONV, k0:k1] * u
        for t in range(FF_CONV - 1):
            off = HIST - (FF_CONV - 1) + t
            uc = uc + fcw_ref[t:t + 1, k0:k1] * u_ext[off:off + tile, k0:k1]
        d = _dot(_bf(_silu(uc) * gate_in), wdn_ref[k0:k1, :])
        val["acc"] = d if "acc" not in val else val["acc"] + d
        yield

    n_chunks = tile // CHUNK
    heads = [head_unit(c, hd) for c in range(n_chunks) for hd in range(N_HEADS)]
    proj_heads = [a_proj(P_HEAD0 + hd * HEAD_COLS, P_HEAD0 + (hd + 1) * HEAD_COLS) for hd in range(N_HEADS)]
    n_ffn = -(-D_FF // FFN_PIECE)
    ffn = None
    big = []
    for k in range(n_ffn + 1):
        big.append(("ffn", k))
    small = []
    a_norm()
    b_conv()
    for c in range(n_chunks):
        b_gates(c)
    c_out()
    ffn = [ffn_unit(k * FFN_PIECE, min((k + 1) * FFN_PIECE, D_FF)) for k in range(n_ffn)]

    def ffn_slot(k):
        if k < n_ffn:
            next(ffn[k])
        if k >= 1:
            next(ffn[k - 1])

    slots = [lambda k=k: ffn_slot(k) for k in range(n_ffn + 1)]
    slots += [a_proj(0, GROUP_W), a_proj(GROUP_W, P_HEAD0)]
    slot_iter = iter(slots)
    tail = list(proj_heads)
    for u, unit in enumerate(heads):
        next(unit)
        next(slot_iter, lambda: None)()
        next(unit)
        next(slot_iter, lambda: None)()
        next(unit)
        if u >= N_HEADS * (n_chunks - 1):
            pass
    for rest in slot_iter:
        rest()
    c_final()
    for piece in tail:
        piece()

    @pl.when((jb == nt - 1) & (i >= 1) & (i <= n_tiles))
    def _():
        mconv_ref[0] = qk_ext[0:HIST, :]
        ret_ref[0] = s_sc[...]
        mc_ref[0] = c_sc[...]
        for hd in range(N_HEADS):
            mn_ref[0, hd:hd + 1, :] = n_sc[hd][0:1, :]
            mm_ref[0, hd:hd + 1, :] = m_sc[hd][0:1, :]

    @pl.when((jc == nt - 1) & (i >= 2))
    def _():
        ffconv_ref[0] = u_ext[0:HIST, :]


def _const_spec(shape):
    nd = len(shape)
    return pl.BlockSpec(shape, lambda *_: (0,) * nd, pipeline_mode=pl.Buffered(1))


def _prompt_call(x, mod, n1, n2, nf, win, gbias, mcw, mcb, rnw, mnw, wout, wup, fcw, fcb, wdn,
                 cos_t, sin_t, consts):
    nb, seq, _ = x.shape
    tile = PROMPT_TILE
    nt = seq // tile
    n_tiles = nb * nt
    inner, qdec, kdec, cdec = consts
    state = (N_HEADS, HEAD, HEAD)

    def tile_a(i):
        return jnp.minimum(i, n_tiles - 1)

    def tile_b(i):
        return jnp.clip(i - 1, 0, n_tiles - 1)

    def tile_c(i):
        return jnp.clip(i - 2, 0, n_tiles - 1)

    in_specs = [
        pl.BlockSpec((1, tile, D_MODEL), lambda s: (tile_a(s) // nt, tile_a(s) % nt, 0)),
        pl.BlockSpec((1, tile, D_MODEL), lambda s: (tile_c(s) // nt, tile_c(s) % nt, 0)),
        pl.BlockSpec((1, 6, D_MODEL), lambda s: (tile_a(s) // nt, 0, 0)),
        pl.BlockSpec((1, 6, D_MODEL), lambda s: (tile_c(s) // nt, 0, 0)),
        _const_spec((1, D_MODEL)), _const_spec((1, D_MODEL)), _const_spec((1, D_MODEL)),
        _const_spec((D_MODEL, IN_PAD)), _const_spec((1, HEAD)),
        _const_spec((M_CONV, 2 * GROUP_W)), _const_spec((1, 2 * GROUP_W)),
        _const_spec((1, GROUP_W)), _const_spec((1, GROUP_W)),
        _const_spec((2 * GROUP_W, D_MODEL)),
        _const_spec((D_MODEL, 2 * D_FF)),
        _const_spec((FF_CONV, D_FF)), _const_spec((1, D_FF)),
        _const_spec((D_FF, D_MODEL)),
        pl.BlockSpec((tile, HEAD), lambda s: (tile_b(s) % nt, 0)),
        pl.BlockSpec((tile, HEAD), lambda s: (tile_b(s) % nt, 0)),
        _const_spec(state), _const_spec(state), _const_spec(state), _const_spec(state),
    ]
    out_specs = [
        pl.BlockSpec((1, tile, D_MODEL), lambda s: (tile_c(s) // nt, tile_c(s) % nt, 0)),
        pl.BlockSpec((1,) + state, lambda s: (tile_b(s) // nt, 0, 0, 0)),
        pl.BlockSpec((1,) + state, lambda s: (tile_b(s) // nt, 0, 0, 0)),
        pl.BlockSpec((1, N_HEADS, HEAD), lambda s: (tile_b(s) // nt, 0, 0)),
        pl.BlockSpec((1, N_HEADS, HEAD), lambda s: (tile_b(s) // nt, 0, 0)),
        pl.BlockSpec((1, HIST, 2 * GROUP_W), lambda s: (tile_b(s) // nt, 0, 0)),
        pl.BlockSpec((1, HIST, D_FF), lambda s: (tile_c(s) // nt, 0, 0)),
    ]
    out_shape = [
        jax.ShapeDtypeStruct((nb, seq, D_MODEL), F32),
        jax.ShapeDtypeStruct((nb,) + state, F32),
        jax.ShapeDtypeStruct((nb,) + state, F32),
        jax.ShapeDtypeStruct((nb, N_HEADS, HEAD), F32),
        jax.ShapeDtypeStruct((nb, N_HEADS, HEAD), F32),
        jax.ShapeDtypeStruct((nb, HIST, 2 * GROUP_W), F32),
        jax.ShapeDtypeStruct((nb, HIST, D_FF), F32),
    ]
    scratch = [
        pltpu.VMEM((tile, IN_PAD), F32),
        pltpu.VMEM((tile + HIST, 2 * GROUP_W), F32),
        pltpu.VMEM((tile, 2 * GROUP_W), F32),
        pltpu.VMEM((tile, D_MODEL), BF16),
        pltpu.VMEM((tile + HIST, D_FF), F32),
        pltpu.VMEM((tile, 2 * GROUP_W), BF16),
        pltpu.VMEM((tile, D_MODEL), F32),
        pltpu.VMEM(state, F32), pltpu.VMEM(state, F32), pltpu.VMEM(state, F32), pltpu.VMEM(state, F32),
    ]
    return pl.pallas_call(
        functools.partial(_prompt_kernel, tile=tile, nt=nt),
        grid=(n_tiles + 2,),
        in_specs=in_specs, out_specs=out_specs, out_shape=out_shape, scratch_shapes=scratch,
        compiler_params=pltpu.CompilerParams(
            dimension_semantics=("arbitrary",),
            vmem_limit_bytes=V7X_VMEM_BYTES - 4 * 1024 * 1024),
        name="prompt_layer",
    )(x, x, mod, mod, n1, n2, nf, win, gbias, mcw, mcb, rnw, mnw, wout, wup, fcw, fcb, wdn,
      cos_t, sin_t, inner, qdec, kdec, cdec)


def _s_in_kernel(x_ref, sh1_ref, sc1_ref, n1_ref, win_ref, o_ref, *, seq):
    a1 = _repeat_rows(n1_ref[...] * (1.0 + sc1_ref[...]), seq)
    h = _bf(_rms(x_ref[...]) * a1 + _repeat_rows(sh1_ref[...], seq))
    o_ref[...] = _dot(h, win_ref[...])


def _s_in_call(xs, mod_s, n1, win, seq):
    rows = xs.shape[0]
    nbatch = rows // seq
    nblk = 3
    wblk = IN_PAD // nblk
    return pl.pallas_call(
        functools.partial(_s_in_kernel, seq=seq),
        grid=(nblk,),
        in_specs=[pl.BlockSpec((rows, D_MODEL), lambda n: (0, 0)),
                  pl.BlockSpec((nbatch, D_MODEL), lambda n: (0, 0)),
                  pl.BlockSpec((nbatch, D_MODEL), lambda n: (0, 1)),
                  pl.BlockSpec((1, D_MODEL), lambda n: (0, 0)),
                  pl.BlockSpec((D_MODEL, wblk), lambda n: (0, n))],
        out_specs=pl.BlockSpec((rows, wblk), lambda n: (0, n)),
        out_shape=jax.ShapeDtypeStruct((rows, IN_PAD), F32),
        name="sample_in",
    )(xs, mod_s, mod_s, n1, win)


def _s_mix_kernel(rq_ref, rk_ref, rv_ref, rg_ref, mq_ref, mk_ref, mv_ref, mo_ref, gate_ref,
                  hq_ref, hk_ref, s0_ref, c0_ref, n0_ref, m0_ref, gb_ref,
                  mcwq_ref, mcwk_ref, mcbq_ref, mcbk_ref, rnw_ref, mnw_ref,
                  cos_ref, sin_ref, inner_ref, qdec_ref, kdec_ref, cdec_ref,
                  yr_ref, ym_ref, s1_ref, c1_ref, n1_ref, m1_ref, o_sc, *, seq):
    hd = pl.program_id(1)
    rows = rq_ref.shape[0]
    nbatch = rows // seq
    ops = _ChunkOps(rows, seq)
    row_t = lax.broadcasted_iota(jnp.int32, (rows, HEAD), 0) & (seq - 1)
    tile_half = lax.broadcasted_iota(jnp.int32, (2 * seq, HEAD), 0) >= seq
    lane_b = lax.broadcasted_iota(jnp.int32, (HEAD, rows), 1) >> (seq.bit_length() - 1)

    def state_read(state_ref):
        def read(qq):
            for i in range(nbatch // 2):
                qt = qq[2 * seq * i:2 * seq * (i + 1), :]
                lo = _dot(_bf(jnp.where(tile_half, 0.0, qt)), _bf(state_ref[2 * i, 0]))
                hi = _dot(_bf(jnp.where(tile_half, qt, 0.0)), _bf(state_ref[2 * i + 1, 0]))
                o_sc[2 * seq * i:2 * seq * (i + 1), :] = lo + hi
            return o_sc[...]
        return read

    def state_write(new_ref, old_ref, decay_rows, kx, v):
        kx_t = kx.T
        vb = _bf(v)
        for b in range(nbatch):
            upd = _dot(_bf(jnp.where(lane_b == b, kx_t, 0.0)), vb)
            dec = jnp.broadcast_to(decay_rows[seq * b:seq * b + 1, :], (HEAD, HEAD))
            new_ref[b, 0] = dec * old_ref[b, 0] + upd

    cos_t, sin_t = cos_ref[...], sin_ref[...]
    q = _rope(rq_ref[...], cos_t, sin_t)
    k = _rope(rk_ref[...], cos_t, sin_t) * (HEAD ** -0.5)
    v = rv_ref[...]
    o, kd = _ret_block(q, k, v, inner_ref[0], qdec_ref[0], kdec_ref[0], state_read(s0_ref))
    state_write(s1_ref, s0_ref, cdec_ref[0], kd, v)
    yr_ref[...] = _bf(_head_norm(o, rnw_ref[...]) * _silu(rg_ref[...]))

    def conv(x, hist, w_ref, b_ref):
        out = b_ref[...] + w_ref[M_CONV - 1:M_CONV, :] * x
        for d in range(1, M_CONV):
            prev = jnp.where(row_t >= d, _roll_rows(x, d), _roll_rows(hist, d - (M_CONV - 1)))
            out = out + w_ref[M_CONV - 1 - d:M_CONV - d, :] * prev
        return out

    mq = _silu(conv(mq_ref[...], hq_ref[...], mcwq_ref, mcbq_ref)) * (HEAD ** -0.5)
    mk = _silu(conv(mk_ref[...], hk_ref[...], mcwk_ref, mcbk_ref))
    mv = mv_ref[...]
    act, csum, act_t, csum_t = _gate_block(gate_ref[...], gb_ref[...], ops)
    m_col = m0_ref[0]
    hm, kw, wc, m_new, n_new = _mlstm_block(
        mq, mk, mv,
        _pick_lane(act, hd), _pick_row(act_t, hd),
        _pick_lane(csum, N_HEADS + hd), _pick_row(csum_t, N_HEADS + hd),
        m_col, n0_ref[...], ops, state_read(c0_ref))
    state_write(c1_ref, c0_ref, wc, kw, mv)
    n1_ref[...] = n_new
    m1_ref[...] = m_new
    ym_ref[...] = _bf(_head_norm(hm, mnw_ref[...]) * _sigmoid(mo_ref[...]))


def _s_mix_call(proj, hist, s0, c0, n_rows, m_rows, gbias, mcw, mcb, rnw, mnw, cos_t, sin_t, consts, seq):
    rows_all = proj.shape[0]
    blk = SAMPLE_GROUP * seq
    ng = rows_all // blk
    inner, qdec, kdec, cdec = consts

    def col(cb):
        return pl.BlockSpec((blk, HEAD), lambda g, h, cb=cb: (g, cb + h))

    def hcol(off):
        return pl.BlockSpec(
            (blk, HEAD), lambda g, h, off=off: (g, (P_HEAD0 + off) // HEAD + h * (HEAD_COLS // HEAD)))

    def head_const(arr_rows):
        return pl.BlockSpec((arr_rows, HEAD), lambda g, h: (0, h))

    st_spec = pl.BlockSpec((SAMPLE_GROUP, 1, HEAD, HEAD), lambda g, h: (g, h, 0, 0))
    hconst = pl.BlockSpec((1, HEAD, HEAD), lambda g, h: (h, 0, 0))
    in_specs = [
        hcol(O_RQ), hcol(O_RK), hcol(O_RV), hcol(O_RG),
        col(P_MQ // HEAD), col(P_MK // HEAD), hcol(O_MV), hcol(O_MO),
        pl.BlockSpec((blk, HEAD), lambda g, h: (g, P_GATE // HEAD)),
        col(0), col(N_HEADS),
        st_spec, st_spec,
        pl.BlockSpec((blk, HEAD), lambda g, h: (g, h)),
        pl.BlockSpec((1, blk, 1), lambda g, h: (h, g, 0)),
        pl.BlockSpec((1, HEAD), lambda g, h: (0, 0)),
        head_const(M_CONV), pl.BlockSpec((M_CONV, HEAD), lambda g, h: (0, N_HEADS + h)),
        head_const(1), pl.BlockSpec((1, HEAD), lambda g, h: (0, N_HEADS + h)),
        head_const(1), head_const(1),
        pl.BlockSpec((blk, HEAD), lambda g, h: (0, 0)),
        pl.BlockSpec((blk, HEAD), lambda g, h: (0, 0)),
        hconst, hconst, hconst, hconst,
    ]
    out_specs = [
        pl.BlockSpec((blk, HEAD), lambda g, h: (g, h)),
        pl.BlockSpec((blk, HEAD), lambda g, h: (g, h)),
        st_spec, st_spec,
        pl.BlockSpec((blk, HEAD), lambda g, h: (g, h)),
        pl.BlockSpec((blk, HEAD), lambda g, h: (g, h)),
    ]
    out_shape = [
        jax.ShapeDtypeStruct((rows_all, GROUP_W), BF16),
        jax.ShapeDtypeStruct((rows_all, GROUP_W), BF16),
        jax.ShapeDtypeStruct(s0.shape, F32),
        jax.ShapeDtypeStruct(c0.shape, F32),
        jax.ShapeDtypeStruct((rows_all, GROUP_W), F32),
        jax.ShapeDtypeStruct((rows_all, GROUP_W), F32),
    ]
    return pl.pallas_call(
        functools.partial(_s_mix_kernel, seq=seq),
        grid=(ng, N_HEADS),
        in_specs=in_specs, out_specs=out_specs, out_shape=out_shape,
        scratch_shapes=[pltpu.VMEM((blk, HEAD), F32)],
        compiler_params=pltpu.CompilerParams(dimension_semantics=("arbitrary", "arbitrary")),
        name="sample_mix",
    )(proj, proj, proj, proj, proj, proj, proj, proj, proj, hist, hist, s0, c0, n_rows, m_rows, gbias,
      mcw, mcw, mcb, mcb, rnw, mnw, cos_t, sin_t, inner, qdec, kdec, cdec)


def _s_out_kernel(x_ref, yr_ref, ym_ref, g1_ref, sh2_ref, sc2_ref, g2_ref, n2_ref, nf_ref,
                  wout_ref, wup_ref, fcw_ref, fcb_ref, wdn_ref, fh_ref,
                  y_ref, u_ref, *, seq):
    rows = x_ref.shape[0]
    row_t = lax.broadcasted_iota(jnp.int32, (rows, D_FF), 0) & (seq - 1)
    mix = _dot(yr_ref[...], wout_ref[0:GROUP_W, :]) + _dot(ym_ref[...], wout_ref[GROUP_W:2 * GROUP_W, :])
    x1 = x_ref[...] + _repeat_rows(g1_ref[...], seq) * mix

    def conv_fn(u):
        hist = fh_ref[...]
        uc = fcb_ref[...] + fcw_ref[FF_CONV - 1:FF_CONV, :] * u
        for d in range(1, FF_CONV):
            prev = jnp.where(row_t >= d, _roll_rows(u, d), _roll_rows(hist, d - (FF_CONV - 1)))
            uc = uc + fcw_ref[FF_CONV - 1 - d:FF_CONV - d, :] * prev
        return uc

    y, u = _ffn(x1, _repeat_rows(n2_ref[...] * (1.0 + sc2_ref[...]), seq), _repeat_rows(sh2_ref[...], seq),
                _repeat_rows(g2_ref[...], seq), nf_ref[...], wup_ref, wdn_ref, conv_fn)
    y_ref[...] = y
    u_ref[...] = u


def _s_out_call(xs, yr, ym, mod_s, n2, nf, wout, wup, fcw, fcb, wdn, fhist, seq):
    rows_all = xs.shape[0]
    blk = S_OUT_ROWS
    nblk = rows_all // blk

    def modcol(cb):
        return pl.BlockSpec((blk // seq, D_MODEL), lambda i, cb=cb: (i, cb))

    in_specs = [
        pl.BlockSpec((blk, D_MODEL), lambda i: (i, 0)),
        pl.BlockSpec((blk, GROUP_W), lambda i: (i, 0)),
        pl.BlockSpec((blk, GROUP_W), lambda i: (i, 0)),
        modcol(2), modcol(3), modcol(4), modcol(5),
        _const_spec((1, D_MODEL)), _const_spec((1, D_MODEL)),
        _const_spec((2 * GROUP_W, D_MODEL)),
        _const_spec((D_MODEL, 2 * D_FF)),
        _const_spec((FF_CONV, D_FF)), _const_spec((1, D_FF)),
        _const_spec((D_FF, D_MODEL)),
        pl.BlockSpec((blk, D_FF), lambda i: (i, 0)),
    ]
    return pl.pallas_call(
        functools.partial(_s_out_kernel, seq=seq),
        grid=(nblk,),
        in_specs=in_specs,
        out_specs=[pl.BlockSpec((blk, D_MODEL), lambda i: (i, 0)),
                   pl.BlockSpec((blk, D_FF), lambda i: (i, 0))],
        out_shape=[jax.ShapeDtypeStruct((rows_all, D_MODEL), F32),
                   jax.ShapeDtypeStruct((rows_all, D_FF), F32)],
        compiler_params=pltpu.CompilerParams(
            dimension_semantics=("arbitrary",),
            vmem_limit_bytes=V7X_VMEM_BYTES - 4 * 1024 * 1024),
        name="sample_out",
    )(xs, yr, ym, mod_s, mod_s, mod_s, mod_s, n2, nf, wout, wup, fcw, fcb, wdn, fhist)


def _rope_tables(pos):
    half = HEAD // 2
    inv = ROPE_THETA ** (-np.arange(half, dtype=np.float64) / half)
    ang = np.asarray(pos, np.float64)[:, None] * inv[None, :]
    cos, sin = np.cos(ang), np.sin(ang)
    return (jnp.asarray(np.concatenate([cos, cos], axis=-1), F32),
            jnp.asarray(np.concatenate([-sin, sin], axis=-1), F32))


def _retention_consts(cc, rows):
    lg = np.log1p(-np.exp2(-5.0 - np.arange(N_HEADS, dtype=np.float64)))
    ridx = np.arange(rows)
    t = (ridx % cc).astype(np.float64)
    rel = t[:, None] - t[None, :]
    same = (ridx[:, None] // cc) == (ridx[None, :] // cc)
    inner = np.where(same & (rel >= 0), np.exp(np.maximum(rel, 0.0) * lg[:, None, None]), 0.0)
    qdec = np.exp((t + 1.0) * lg[:, None])
    kdec = np.exp((cc - 1.0 - t) * lg[:, None])
    cdec = np.exp(cc * lg)
    wide = (N_HEADS, rows, HEAD)
    return tuple(jnp.asarray(a, F32) for a in (
        inner,
        np.broadcast_to(qdec[:, :, None], wide),
        np.broadcast_to(kdec[:, :, None], wide),
        np.broadcast_to(cdec[:, None, None], wide)))


def _regroup_kernel(w_ref, o_ref):
    gw = GROUP_W
    o_ref[:, P_MQ:P_GATE] = _bf(w_ref[:, 4 * gw:6 * gw])
    o_ref[:, P_GATE:P_HEAD0] = jnp.zeros((o_ref.shape[0], HEAD), BF16)
    o_ref[:, P_GATE:P_GATE + 2 * N_HEADS] = _bf(w_ref[:, 8 * gw:8 * gw + 2 * N_HEADS])
    for hd in range(N_HEADS):
        for n, grp in enumerate((0, 1, 2, 3, 6, 7)):
            dst = P_HEAD0 + hd * HEAD_COLS + n * HEAD
            src = grp * gw + hd * HEAD
            o_ref[:, dst:dst + HEAD] = _bf(w_ref[:, src:src + HEAD])


def _regroup_w_in(w):
    d, cols = w.shape
    rows = 256
    return pl.pallas_call(
        _regroup_kernel,
        grid=(d // rows,),
        in_specs=[pl.BlockSpec((rows, cols), lambda r: (r, 0))],
        out_specs=pl.BlockSpec((rows, IN_PAD), lambda r: (r, 0)),
        out_shape=jax.ShapeDtypeStruct((d, IN_PAD), BF16),
        name="regroup_w_in",
    )(w)


def kernel(x_prompt, x_sample, c_prompt, c_sample, state_ret, state_mlstm_C, state_mlstm_n, state_mlstm_m, state_mconv, state_ffconv, w_ada, b_ada, norm1_w, norm2_w, w_in, b_igate, b_fgate, mconv_w, mconv_b, ret_norm_w, mlstm_norm_w, w_out, w_up, ffconv_w, ffconv_b, w_down, final_w):
    assert w_ada.shape[0] == 1, "single-layer kernel"
    nb, seq_p, _ = x_prompt.shape
    ns, seq_s, _ = x_sample.shape
    assert seq_p % PROMPT_TILE == 0 and seq_s == 4 and ns % SAMPLE_GROUP == 0
    rows_s = ns * seq_s

    win = _regroup_w_in(w_in[0])
    wout = _bf(w_out[0])
    wup = _bf(w_up[0])
    wdn = _bf(w_down[0])
    gbias = jnp.pad(jnp.concatenate([b_igate[0], b_fgate[0]]), (0, HEAD - 2 * N_HEADS))[None, :]
    n1, n2, nf = norm1_w, norm2_w, final_w[None, :]
    mcw, mcb = mconv_w[0], mconv_b
    rnw, mnw = ret_norm_w, mlstm_norm_w
    fcw, fcb = ffconv_w[0], ffconv_b

    mod = _ada_call(jnp.concatenate([c_prompt, c_sample], axis=0), w_ada[0], b_ada)
    mod_p = mod[:nb].reshape(nb, 6, D_MODEL)
    mod_s = mod[nb:]

    cos_p, sin_p = _rope_tables(np.arange(seq_p))
    y_p, ret_p, mc_p, mn_p, mm_p, mconv_t, ffconv_t = _prompt_call(
        x_prompt, mod_p, n1, n2, nf, win, gbias, mcw, mcb, rnw, mnw, wout, wup, fcw, fcb, wdn,
        cos_p, sin_p, _retention_consts(CHUNK, CHUNK))
    mconv_p = mconv_t[:, HIST - (M_CONV - 1):, :]
    ffconv_p = ffconv_t[:, HIST - (FF_CONV - 1):, :]

    xs = x_sample.reshape(rows_s, D_MODEL)
    proj_s = _s_in_call(xs, mod_s, n1, win, seq_s)
    blk = SAMPLE_GROUP * seq_s
    cos_s, sin_s = _rope_tables(PAST_LEN + (np.arange(blk) % seq_s))
    hist_m = jnp.pad(state_mconv[0], ((0, 0), (0, seq_s - (M_CONV - 1)), (0, 0))).reshape(rows_s, 2 * GROUP_W)
    n_rows = jnp.repeat(state_mlstm_n[0].reshape(ns, GROUP_W), seq_s, axis=0)
    m_rows = jnp.repeat(state_mlstm_m[0], seq_s, axis=0).T[:, :, None]
    yr, ym, ret_s, mc_s, n_new, m_new = _s_mix_call(
        proj_s, hist_m, state_ret[0], state_mlstm_C[0], n_rows, m_rows, gbias, mcw, mcb, rnw, mnw,
        cos_s, sin_s, _retention_consts(seq_s, blk), seq_s)
    hist_f = jnp.pad(state_ffconv[0], ((0, 0), (0, seq_s - (FF_CONV - 1)), (0, 0))).reshape(rows_s, D_FF)
    y_s, u_s = _s_out_call(xs, yr, ym, mod_s, n2, nf, wout, wup, fcw, fcb, wdn, hist_f, seq_s)

    mn_s = n_new[::seq_s].reshape(ns, N_HEADS, HEAD)
    mm_s = m_new[::seq_s].reshape(ns, N_HEADS, HEAD)[:, :, 0]
    mconv_s = jnp.stack([proj_s[t::seq_s, P_MQ:P_GATE] for t in range(seq_s - (M_CONV - 1), seq_s)], axis=1)
    ffconv_s = jnp.stack([u_s[t::seq_s, :] for t in range(seq_s - (FF_CONV - 1), seq_s)], axis=1)

    return (y_p, y_s.reshape(ns, seq_s, D_MODEL),
            ret_p[None], mc_p[None], mn_p[None], mm_p[:, :, 0][None], mconv_p[None], ffconv_p[None],
            ret_s[None], mc_s[None], mn_s[None], mm_s[None], mconv_s[None], ffconv_s[None])
```

```python
import functools

import jax
import jax.numpy as jnp
import numpy as np
from jax import lax
from jax.experimental import pallas as pl
from jax.experimental.pallas import tpu as pltpu

F32 = jnp.float32
BF16 = jnp.bfloat16

D_MODEL = 1024
N_HEADS = 4
HEAD = 128
GROUP_W = N_HEADS * HEAD
D_FF = 2816
M_CONV = 4
FF_CONV = 3
CHUNK = 128
PAST_LEN = 16384
ROPE_THETA = 10000.0
RMS_EPS = 1e-6
GN_EPS = 1e-5
NEG = -1e30

P_MQ, P_MK, P_GATE = 0, GROUP_W, 2 * GROUP_W
P_HEAD0 = 2 * GROUP_W + HEAD
HEAD_COLS = 6 * HEAD
O_RQ, O_RK, O_RV, O_RG, O_MV, O_MO = (i * HEAD for i in range(6))
IN_PAD = P_HEAD0 + N_HEADS * HEAD_COLS
FFN_PIECE = 256
HIST = 8

V7X_VMEM_BYTES = 64 * 1024 * 1024
PROMPT_TILE = 256
SAMPLE_GROUP = 32
S_OUT_ROWS = 256


def _dot(a, b):
    return jnp.dot(a, b, preferred_element_type=F32)


def _dot_nt(a, b):
    return lax.dot_general(a, b, (((1,), (1,)), ((), ())), preferred_element_type=F32)


def _dot_tn(a, b):
    return lax.dot_general(a, b, (((0,), (0,)), ((), ())), preferred_element_type=F32)


def _bf(x):
    return x.astype(BF16)


def _silu(x):
    return x * (1.0 / (1.0 + jnp.exp(-x)))


def _sigmoid(x):
    return 1.0 / (1.0 + jnp.exp(-x))


def _log_sigmoid(x):
    return jnp.minimum(x, 0.0) - jnp.log(1.0 + jnp.exp(-jnp.abs(x)))


def _rms(x):
    return x * lax.rsqrt(jnp.mean(x * x, axis=-1, keepdims=True) + RMS_EPS)


def _head_norm(x, w_row):
    mu = jnp.mean(x, axis=-1, keepdims=True)
    xc = x - mu
    var = jnp.mean(xc * xc, axis=-1, keepdims=True)
    return xc * lax.rsqrt(var + GN_EPS) * w_row


def _rope(x, cos_t, sin_t):
    return x * cos_t + pltpu.roll(x, HEAD // 2, 1) * sin_t


def _pick_lane(x, idx):
    lane = lax.broadcasted_iota(jnp.int32, x.shape, 1)
    return jnp.sum(jnp.where(lane == idx, x, 0.0), axis=1, keepdims=True)


def _pick_row(x, idx):
    row = lax.broadcasted_iota(jnp.int32, x.shape, 0)
    return jnp.sum(jnp.where(row == idx, x, 0.0), axis=0, keepdims=True)


def _split3(x):
    hi = _bf(x)
    r = x - hi.astype(F32)
    mid = _bf(r)
    lo = _bf(r - mid.astype(F32))
    return hi, mid, lo


def _roll_rows(x, shift):
    shift = shift % x.shape[0]
    return x if shift == 0 else pltpu.roll(x, shift, 0)


class _ChunkOps:
    def __init__(self, rows, seq):
        self.rows, self.seq = rows, seq
        r = lax.broadcasted_iota(jnp.int32, (rows, rows), 0)
        c = lax.broadcasted_iota(jnp.int32, (rows, rows), 1)
        if seq == rows:
            self.causal = c <= r
            self.tril = jnp.where(self.causal, 1.0, 0.0).astype(BF16)
        else:
            shift = seq.bit_length() - 1
            same = (r >> shift) == (c >> shift)
            self.causal = same & (c <= r)
            self.t = lax.broadcasted_iota(jnp.int32, (rows, HEAD), 0) & (seq - 1)

    def cumsum(self, x):
        if self.seq == self.rows:
            hi, mid, lo = _split3(x)
            return _dot(self.tril, hi) + _dot(self.tril, mid) + _dot(self.tril, lo)
        out = x
        for d in range(1, self.seq):
            out = out + jnp.where(self.t >= d, _roll_rows(x, d), 0.0)
        return out

    def last(self, x):
        if self.seq == self.rows:
            return jnp.broadcast_to(x[self.rows - 1:self.rows, :], x.shape)
        y = jnp.where(self.t == self.seq - 1, x, 0.0)
        out = y
        for d in range(1, self.seq):
            out = out + _roll_rows(y, -d)
        return out

    def total(self, x):
        if self.seq == self.rows:
            return jnp.broadcast_to(jnp.sum(x, axis=0, keepdims=True), x.shape)
        assert self.seq == 4
        odd = (self.t & 1) == 1
        p = x + jnp.where(odd, _roll_rows(x, 1), _roll_rows(x, -1))
        return p + jnp.where(self.t >= 2, _roll_rows(p, 2), _roll_rows(p, -2))


def _repeat_rows(x, seq):
    n = x.shape[0]
    r = lax.broadcasted_iota(jnp.int32, (n * seq, n), 0)
    c = lax.broadcasted_iota(jnp.int32, (n * seq, n), 1)
    sel = jnp.where((r >> (seq.bit_length() - 1)) == c, 1.0, 0.0).astype(BF16)
    hi, mid, lo = _split3(x)
    return _dot(sel, hi) + _dot(sel, mid) + _dot(sel, lo)


def _gate_block(gates, gbias, ops):
    z = gates + gbias
    lane = lax.broadcasted_iota(jnp.int32, z.shape, 1)
    act = jnp.where(lane < N_HEADS, z, _log_sigmoid(z))
    csum = ops.cumsum(act)
    return act, csum, act.T, csum.T


def _ret_block(q, k, v, inner, qdec, kdec, state_read):
    s = _dot_nt(_bf(q), _bf(k)) * inner
    o = _dot(_bf(s), _bf(v)) + state_read(q * qdec)
    return o, k * kdec


def _mlstm_block(q, k, v, ig_col, ig_row, b_col, b_row, m_col, n_rows, ops, state_read):
    rows = q.shape[0]
    logw = jnp.where(ops.causal, b_col - b_row + ig_row, NEG)
    inter = b_col + m_col
    mt = jnp.maximum(inter, jnp.max(logw, axis=-1, keepdims=True))
    s = _dot_nt(_bf(q), _bf(k)) * jnp.exp(logw - mt)
    wi = jnp.exp(inter - mt)
    num = wi * state_read(q) + _dot(_bf(s), _bf(v))
    den = wi * jnp.sum(q * n_rows, axis=-1, keepdims=True) + jnp.sum(s, axis=-1, keepdims=True)
    h = num / jnp.maximum(jnp.abs(den), jnp.exp(-mt))
    b_last = ops.last(jnp.broadcast_to(b_col, (rows, HEAD)))
    m_new = ops.last(jnp.broadcast_to(mt, (rows, HEAD)))
    wk = jnp.exp(b_last - b_col + ig_col - m_new)
    wc = jnp.exp(b_last + m_col - m_new)
    kw = k * wk
    n_new = wc * n_rows + ops.total(kw)
    return h, kw, wc, m_new, n_new


def _ada_kernel(c_ref, w_ref, b_ref, o_ref):
    c = c_ref[...]
    o_ref[...] = _dot(_bf(_silu(c)), _bf(w_ref[...])) + b_ref[...]


def _ada_call(c_all, w_ada, b_ada):
    rows = c_all.shape[0]
    n_blk = w_ada.shape[1] // D_MODEL
    return pl.pallas_call(
        _ada_kernel,
        grid=(n_blk,),
        in_specs=[pl.BlockSpec((rows, D_MODEL), lambda n: (0, 0)),
                  pl.BlockSpec((D_MODEL, D_MODEL), lambda n: (0, n)),
                  pl.BlockSpec((1, D_MODEL), lambda n: (0, n))],
        out_specs=pl.BlockSpec((rows, D_MODEL), lambda n: (0, n)),
        out_shape=jax.ShapeDtypeStruct((rows, w_ada.shape[1]), F32),
        name="ada",
    )(c_all, w_ada, b_ada)


def _ffn(x1, a2, sh2, g2, nf, wup_ref, wdn_ref, conv_fn):
    h2 = _bf(_rms(x1) * a2 + sh2)
    u = _dot(h2, wup_ref[:, 0:D_FF])
    val = _dot(h2, wup_ref[:, D_FF:2 * D_FF])
    uc = conv_fn(u)
    act = _bf(_silu(uc) * val)
    x2 = x1 + g2 * _dot(act, wdn_ref[...])
    return _rms(x2) * nf, u


def _prompt_kernel(xa_ref, xc_ref, moda_ref, modc_ref, n1_ref, n2_ref, nf_ref, win_ref, gb_ref,
                   mcw_ref, mcb_ref, rnw_ref, mnw_ref, wout_ref, wup_ref, fcw_ref, fcb_ref, wdn_ref,
                   cos_ref, sin_ref, inner_ref, qdec_ref, kdec_ref, cdec_ref,
                   y_ref, ret_ref, mc_ref, mn_ref, mm_ref, mconv_ref, ffconv_ref,
                   proj_sc, qk_ext, qkc_sc, h_sc, u_ext, ymix_sc, x1_sc, s_sc, c_sc, n_sc, m_sc, *, tile, nt):
    i = pl.program_id(0)
    n_tiles = pl.num_programs(0) - 2
    jb = lax.rem(jnp.clip(i - 1, 0, n_tiles - 1), nt)
    jc = lax.rem(jnp.clip(i - 2, 0, n_tiles - 1), nt)

    @pl.when(i == 0)
    def _():
        proj_sc[...] = jnp.zeros_like(proj_sc)
        ymix_sc[...] = jnp.zeros_like(ymix_sc)

    @pl.when(jb == 0)
    def _():
        qk_ext[0:HIST, :] = jnp.zeros((HIST, 2 * GROUP_W), F32)
        s_sc[...] = jnp.zeros_like(s_sc)
        c_sc[...] = jnp.zeros_like(c_sc)
        n_sc[...] = jnp.zeros_like(n_sc)
        m_sc[...] = jnp.zeros_like(m_sc)

    @pl.when(jc == 0)
    def _():
        u_ext[0:HIST, :] = jnp.zeros((HIST, D_FF), F32)

    val = {}
    ops = _ChunkOps(CHUNK, CHUNK)

    def c_out():
        g1 = modc_ref[0, 2:3, :]
        sh2, sc2 = modc_ref[0, 3:4, :], modc_ref[0, 4:5, :]
        x1 = xc_ref[0] + g1 * _dot(ymix_sc[...], wout_ref[...])
        x1_sc[...] = x1
        val["h2"] = _bf(_rms(x1) * (n2_ref[...] * (1.0 + sc2)) + sh2)

    def c_ffn(k0, k1):
        def piece():
            h2 = val["h2"]
            u = _dot(h2, wup_ref[:, k0:k1])
            gate_in = _dot(h2, wup_ref[:, D_FF + k0:D_FF + k1])
            u_ext[HIST:HIST + tile, k0:k1] = u
            uc = fcb_ref[:, k0:k1] + fcw_ref[FF_CONV - 1:FF_CONV, k0:k1] * u
            for t in range(FF_CONV - 1):
                off = HIST - (FF_CONV - 1) + t
                uc = uc + fcw_ref[t:t + 1, k0:k1] * u_ext[off:off + tile, k0:k1]
            d = _dot(_bf(_silu(uc) * gate_in), wdn_ref[k0:k1, :])
            val["acc"] = d if "acc" not in val else val["acc"] + d
        return piece

    def c_final():
        g2 = modc_ref[0, 5:6, :]
        y_ref[0] = _rms(x1_sc[...] + g2 * val["acc"]) * nf_ref[...]
        u_ext[0:HIST, :] = u_ext[tile:tile + HIST, :]

    def a_norm():
        sh1, sc1 = moda_ref[0, 0:1, :], moda_ref[0, 1:2, :]
        h_sc[...] = _bf(_rms(xa_ref[0]) * (n1_ref[...] * (1.0 + sc1)) + sh1)

    def a_proj(c0, c1):
        def piece():
            proj_sc[:, c0:c1] = _dot(h_sc[...], win_ref[:, c0:c1])
        return piece

    def b_conv():
        qk_ext[HIST:HIST + tile, :] = proj_sc[:, P_MQ:P_GATE]
        conv = mcb_ref[...] + mcw_ref[M_CONV - 1:M_CONV, :] * qk_ext[HIST:HIST + tile, :]
        for t in range(M_CONV - 1):
            off = HIST - (M_CONV - 1) + t
            conv = conv + mcw_ref[t:t + 1, :] * qk_ext[off:off + tile, :]
        qkc_sc[...] = _silu(conv)
        qk_ext[0:HIST, :] = qk_ext[tile:tile + HIST, :]

    def b_gates(c):
        val["gate", c] = _gate_block(proj_sc[c * CHUNK:(c + 1) * CHUNK, P_GATE:P_HEAD0], gb_ref[...], ops)

    def head_unit(c, hd):
        rows = slice(c * CHUNK, (c + 1) * CHUNK)
        cols = slice(hd * HEAD, (hd + 1) * HEAD)
        base = P_HEAD0 + hd * HEAD_COLS

        def pcol(off):
            return proj_sc[rows, base + off:base + off + HEAD]

        cos_t, sin_t = cos_ref[rows, :], sin_ref[rows, :]
        q = _rope(pcol(O_RQ), cos_t, sin_t)
        k = _rope(pcol(O_RK), cos_t, sin_t) * (HEAD ** -0.5)
        v = pcol(O_RV)
        s_ret = _dot_nt(_bf(q), _bf(k))
        qs = _dot(_bf(q * qdec_ref[hd]), _bf(s_sc[hd]))
        mq = qkc_sc[rows, hd * HEAD:(hd + 1) * HEAD] * (HEAD ** -0.5)
        mk = qkc_sc[rows, GROUP_W + hd * HEAD:GROUP_W + (hd + 1) * HEAD]
        mv = pcol(O_MV)
        s_ml = _dot_nt(_bf(mq), _bf(mk))
        qc = _dot(_bf(mq), _bf(c_sc[hd]))
        out_gate_r = _silu(pcol(O_RG))
        out_gate_m = _sigmoid(pcol(O_MO))
        yield
        o = _dot(_bf(s_ret * inner_ref[hd]), _bf(v)) + qs
        s_upd = _dot_tn(_bf(k * kdec_ref[hd]), _bf(v))
        act, csum, act_t, csum_t = val["gate", c]
        ig_col, ig_row = _pick_lane(act, hd), _pick_row(act_t, hd)
        b_col, b_row = _pick_lane(csum, N_HEADS + hd), _pick_row(csum_t, N_HEADS + hd)
        m_col, n_rows = m_sc[hd][:, 0:1], n_sc[hd]
        logw = jnp.where(ops.causal, b_col - b_row + ig_row, NEG)
        inter = b_col + m_col
        mt = jnp.maximum(inter, jnp.max(logw, axis=-1, keepdims=True))
        sm = s_ml * jnp.exp(logw - mt)
        wi = jnp.exp(inter - mt)
        num = wi * qc + _dot(_bf(sm), _bf(mv))
        den = wi * jnp.sum(mq * n_rows, axis=-1, keepdims=True) + jnp.sum(sm, axis=-1, keepdims=True)
        b_last = ops.last(jnp.broadcast_to(b_col, (CHUNK, HEAD)))
        m_new = ops.last(jnp.broadcast_to(mt, (CHUNK, HEAD)))
        wk = jnp.exp(b_last - b_col + ig_col - m_new)
        wc = jnp.exp(b_last + m_col - m_new)
        kw = mk * wk
        c_upd = _dot_tn(_bf(kw), _bf(mv))
        yield
        s_sc[hd] = cdec_ref[hd] * s_sc[hd] + s_upd
        ymix_sc[rows, cols] = _bf(_head_norm(o, rnw_ref[:, cols]) * out_gate_r)
        c_sc[hd] = wc * c_sc[hd] + c_upd
        n_sc[hd] = wc * n_rows + ops.total(kw)
        m_sc[hd] = m_new
        hm = num / jnp.maximum(jnp.abs(den), jnp.exp(-mt))
        ymix_sc[rows, GROUP_W + hd * HEAD:GROUP_W + (hd + 1) * HEAD] = _bf(
            _head_norm(hm, mnw_ref[:, cols]) * out_gate_m)
        yield

    def ffn_unit(k0, k1):
        h2 = val["h2"]
        u = _dot(h2, wup_ref[:, k0:k1])
        gate_in = _dot(h2, wup_ref[:, D_FF + k0:D_FF + k1])
        yield
        u_ext[HIST:HIST + tile, k0:k1] = u
        uc = fcb_ref[:, k0:k1] + fcw_ref[FF_CONV - 1:FF_CONV, k0:k1] * u
        for t in range(FF_CONV - 1):
            off = HIST - (FF_CONV - 1) + t
            uc = uc + fcw_ref[t:t + 1, k0:k1] * u_ext[off:off + tile, k0:k1]
        d = _dot(_bf(_silu(uc) * gate_in), wdn_ref[k0:k1, :])
        val["acc"] = d if "acc" not in val else val["acc"] + d
        yield

    n_chunks = tile // CHUNK
    heads = [head_unit(c, hd) for c in range(n_chunks) for hd in range(N_HEADS)]
    proj_heads = [a_proj(P_HEAD0 + hd * HEAD_COLS, P_HEAD0 + (hd + 1) * HEAD_COLS) for hd in range(N_HEADS)]
    n_ffn = -(-D_FF // FFN_PIECE)
    ffn = None
    big = []
    for k in range(n_ffn + 1):
        big.append(("ffn", k))
    small = []
    a_norm()
    b_conv()
    for c in range(n_chunks):
        b_gates(c)
    c_out()
    ffn = [ffn_unit(k * FFN_PIECE, min((k + 1) * FFN_PIECE, D_FF)) for k in range(n_ffn)]

    def ffn_slot(k):
        if k < n_ffn:
            next(ffn[k])
        if k >= 1:
            next(ffn[k - 1])

    slots = [lambda k=k: ffn_slot(k) for k in range(n_ffn + 1)]
    slots += [a_proj(0, GROUP_W), a_proj(GROUP_W, P_HEAD0)]
    slot_iter = iter(slots)
    tail = list(proj_heads)
    for u, unit in enumerate(heads):
        next(unit)
        next(slot_iter, lambda: None)()
        next(unit)
        next(slot_iter, lambda: None)()
        next(unit)
        if u >= N_HEADS * (n_chunks - 1):
            pass
    for rest in slot_iter:
        rest()
    c_final()
    for piece in tail:
        piece()

    @pl.when((jb == nt - 1) & (i >= 1) & (i <= n_tiles))
    def _():
        mconv_ref[0] = qk_ext[0:HIST, :]
        ret_ref[0] = s_sc[...]
        mc_ref[0] = c_sc[...]
        for hd in range(N_HEADS):
            mn_ref[0, hd:hd + 1, :] = n_sc[hd][0:1, :]
            mm_ref[0, hd:hd + 1, :] = m_sc[hd][0:1, :]

    @pl.when((jc == nt - 1) & (i >= 2))
    def _():
        ffconv_ref[0] = u_ext[0:HIST, :]


def _const_spec(shape):
    nd = len(shape)
    return pl.BlockSpec(shape, lambda *_: (0,) * nd, pipeline_mode=pl.Buffered(1))


def _prompt_call(x, mod, n1, n2, nf, win, gbias, mcw, mcb, rnw, mnw, wout, wup, fcw, fcb, wdn,
                 cos_t, sin_t, consts):
    nb, seq, _ = x.shape
    tile = PROMPT_TILE
    nt = seq // tile
    n_tiles = nb * nt
    inner, qdec, kdec, cdec = consts
    state = (N_HEADS, HEAD, HEAD)

    def tile_a(i):
        return jnp.minimum(i, n_tiles - 1)

    def tile_b(i):
        return jnp.clip(i - 1, 0, n_tiles - 1)

    def tile_c(i):
        return jnp.clip(i - 2, 0, n_tiles - 1)

    in_specs = [
        pl.BlockSpec((1, tile, D_MODEL), lambda s: (tile_a(s) // nt, tile_a(s) % nt, 0)),
        pl.BlockSpec((1, tile, D_MODEL), lambda s: (tile_c(s) // nt, tile_c(s) % nt, 0)),
        pl.BlockSpec((1, 6, D_MODEL), lambda s: (tile_a(s) // nt, 0, 0)),
        pl.BlockSpec((1, 6, D_MODEL), lambda s: (tile_c(s) // nt, 0, 0)),
        _const_spec((1, D_MODEL)), _const_spec((1, D_MODEL)), _const_spec((1, D_MODEL)),
        _const_spec((D_MODEL, IN_PAD)), _const_spec((1, HEAD)),
        _const_spec((M_CONV, 2 * GROUP_W)), _const_spec((1, 2 * GROUP_W)),
        _const_spec((1, GROUP_W)), _const_spec((1, GROUP_W)),
        _const_spec((2 * GROUP_W, D_MODEL)),
        _const_spec((D_MODEL, 2 * D_FF)),
        _const_spec((FF_CONV, D_FF)), _const_spec((1, D_FF)),
        _const_spec((D_FF, D_MODEL)),
        pl.BlockSpec((tile, HEAD), lambda s: (tile_b(s) % nt, 0)),
        pl.BlockSpec((tile, HEAD), lambda s: (tile_b(s) % nt, 0)),
        _const_spec(state), _const_spec(state), _const_spec(state), _const_spec(state),
    ]
    out_specs = [
        pl.BlockSpec((1, tile, D_MODEL), lambda s: (tile_c(s) // nt, tile_c(s) % nt, 0)),
        pl.BlockSpec((1,) + state, lambda s: (tile_b(s) // nt, 0, 0, 0)),
        pl.BlockSpec((1,) + state, lambda s: (tile_b(s) // nt, 0, 0, 0)),
        pl.BlockSpec((1, N_HEADS, HEAD), lambda s: (tile_b(s) // nt, 0, 0)),
        pl.BlockSpec((1, N_HEADS, HEAD), lambda s: (tile_b(s) // nt, 0, 0)),
        pl.BlockSpec((1, HIST, 2 * GROUP_W), lambda s: (tile_b(s) // nt, 0, 0)),
        pl.BlockSpec((1, HIST, D_FF), lambda s: (tile_c(s) // nt, 0, 0)),
    ]
    out_shape = [
        jax.ShapeDtypeStruct((nb, seq, D_MODEL), F32),
        jax.ShapeDtypeStruct((nb,) + state, F32),
        jax.ShapeDtypeStruct((nb,) + state, F32),
        jax.ShapeDtypeStruct((nb, N_HEADS, HEAD), F32),
        jax.ShapeDtypeStruct((nb, N_HEADS, HEAD), F32),
        jax.ShapeDtypeStruct((nb, HIST, 2 * GROUP_W), F32),
        jax.ShapeDtypeStruct((nb, HIST, D_FF), F32),
    ]
    scratch = [
        pltpu.VMEM((tile, IN_PAD), F32),
        pltpu.VMEM((tile + HIST, 2 * GROUP_W), F32),
        pltpu.VMEM((tile, 2 * GROUP_W), F32),
        pltpu.VMEM((tile, D_MODEL), BF16),
        pltpu.VMEM((tile + HIST, D_FF), F32),
        pltpu.VMEM((tile, 2 * GROUP_W), BF16),
        pltpu.VMEM((tile, D_MODEL), F32),
        pltpu.VMEM(state, F32), pltpu.VMEM(state, F32), pltpu.VMEM(state, F32), pltpu.VMEM(state, F32),
    ]
    return pl.pallas_call(
        functools.partial(_prompt_kernel, tile=tile, nt=nt),
        grid=(n_tiles + 2,),
        in_specs=in_specs, out_specs=out_specs, out_shape=out_shape, scratch_shapes=scratch,
        compiler_params=pltpu.CompilerParams(
            dimension_semantics=("arbitrary",),
            vmem_limit_bytes=V7X_VMEM_BYTES - 4 * 1024 * 1024),
        name="prompt_layer",
    )(x, x, mod, mod, n1, n2, nf, win, gbias, mcw, mcb, rnw, mnw, wout, wup, fcw, fcb, wdn,
      cos_t, sin_t, inner, qdec, kdec, cdec)


def _s_in_kernel(x_ref, sh1_ref, sc1_ref, n1_ref, win_ref, o_ref, *, seq):
    a1 = _repeat_rows(n1_ref[...] * (1.0 + sc1_ref[...]), seq)
    h = _bf(_rms(x_ref[...]) * a1 + _repeat_rows(sh1_ref[...], seq))
    o_ref[...] = _dot(h, win_ref[...])


def _s_in_call(xs, mod_s, n1, win, seq):
    rows = xs.shape[0]
    nbatch = rows // seq
    nblk = 3
    wblk = IN_PAD // nblk
    return pl.pallas_call(
        functools.partial(_s_in_kernel, seq=seq),
        grid=(nblk,),
        in_specs=[pl.BlockSpec((rows, D_MODEL), lambda n: (0, 0)),
                  pl.BlockSpec((nbatch, D_MODEL), lambda n: (0, 0)),
                  pl.BlockSpec((nbatch, D_MODEL), lambda n: (0, 1)),
                  pl.BlockSpec((1, D_MODEL), lambda n: (0, 0)),
                  pl.BlockSpec((D_MODEL, wblk), lambda n: (0, n))],
        out_specs=pl.BlockSpec((rows, wblk), lambda n: (0, n)),
        out_shape=jax.ShapeDtypeStruct((rows, IN_PAD), F32),
        name="sample_in",
    )(xs, mod_s, mod_s, n1, win)


def _s_mix_kernel(rq_ref, rk_ref, rv_ref, rg_ref, mq_ref, mk_ref, mv_ref, mo_ref, gate_ref,
                  hq_ref, hk_ref, s0_ref, c0_ref, n0_ref, m0_ref, gb_ref,
                  mcwq_ref, mcwk_ref, mcbq_ref, mcbk_ref, rnw_ref, mnw_ref,
                  cos_ref, sin_ref, inner_ref, qdec_ref, kdec_ref, cdec_ref,
                  yr_ref, ym_ref, s1_ref, c1_ref, n1_ref, m1_ref, o_sc, *, seq):
    hd = pl.program_id(1)
    rows = rq_ref.shape[0]
    nbatch = rows // seq
    ops = _ChunkOps(rows, seq)
    row_t = lax.broadcasted_iota(jnp.int32, (rows, HEAD), 0) & (seq - 1)
    tile_half = lax.broadcasted_iota(jnp.int32, (2 * seq, HEAD), 0) >= seq
    lane_b = lax.broadcasted_iota(jnp.int32, (HEAD, rows), 1) >> (seq.bit_length() - 1)

    def state_read(state_ref):
        def read(qq):
            for i in range(nbatch // 2):
                qt = qq[2 * seq * i:2 * seq * (i + 1), :]
                lo = _dot(_bf(jnp.where(tile_half, 0.0, qt)), _bf(state_ref[2 * i, 0]))
                hi = _dot(_bf(jnp.where(tile_half, qt, 0.0)), _bf(state_ref[2 * i + 1, 0]))
                o_sc[2 * seq * i:2 * seq * (i + 1), :] = lo + hi
            return o_sc[...]
        return read

    def state_write(new_ref, old_ref, decay_rows, kx, v):
        kx_t = kx.T
        vb = _bf(v)
        for b in range(nbatch):
            upd = _dot(_bf(jnp.where(lane_b == b, kx_t, 0.0)), vb)
            dec = jnp.broadcast_to(decay_rows[seq * b:seq * b + 1, :], (HEAD, HEAD))
            new_ref[b, 0] = dec * old_ref[b, 0] + upd

    cos_t, sin_t = cos_ref[...], sin_ref[...]
    q = _rope(rq_ref[...], cos_t, sin_t)
    k = _rope(rk_ref[...], cos_t, sin_t) * (HEAD ** -0.5)
    v = rv_ref[...]
    o, kd = _ret_block(q, k, v, inner_ref[0], qdec_ref[0], kdec_ref[0], state_read(s0_ref))
    state_write(s1_ref, s0_ref, cdec_ref[0], kd, v)
    yr_ref[...] = _bf(_head_norm(o, rnw_ref[...]) * _silu(rg_ref[...]))

    def conv(x, hist, w_ref, b_ref):
        out = b_ref[...] + w_ref[M_CONV - 1:M_CONV, :] * x
        for d in range(1, M_CONV):
            prev = jnp.where(row_t >= d, _roll_rows(x, d), _roll_rows(hist, d - (M_CONV - 1)))
            out = out + w_ref[M_CONV - 1 - d:M_CONV - d, :] * prev
        return out

    mq = _silu(conv(mq_ref[...], hq_ref[...], mcwq_ref, mcbq_ref)) * (HEAD ** -0.5)
    mk = _silu(conv(mk_ref[...], hk_ref[...], mcwk_ref, mcbk_ref))
    mv = mv_ref[...]
    act, csum, act_t, csum_t = _gate_block(gate_ref[...], gb_ref[...], ops)
    m_col = m0_ref[0]
    hm, kw, wc, m_new, n_new = _mlstm_block(
        mq, mk, mv,
        _pick_lane(act, hd), _pick_row(act_t, hd),
        _pick_lane(csum, N_HEADS + hd), _pick_row(csum_t, N_HEADS + hd),
        m_col, n0_ref[...], ops, state_read(c0_ref))
    state_write(c1_ref, c0_ref, wc, kw, mv)
    n1_ref[...] = n_new
    m1_ref[...] = m_new
    ym_ref[...] = _bf(_head_norm(hm, mnw_ref[...]) * _sigmoid(mo_ref[...]))


def _s_mix_call(proj, hist, s0, c0, n_rows, m_rows, gbias, mcw, mcb, rnw, mnw, cos_t, sin_t, consts, seq):
    rows_all = proj.shape[0]
    blk = SAMPLE_GROUP * seq
    ng = rows_all // blk
    inner, qdec, kdec, cdec = consts

    def col(cb):
        return pl.BlockSpec((blk, HEAD), lambda g, h, cb=cb: (g, cb + h))

    def hcol(off):
        return pl.BlockSpec(
            (blk, HEAD), lambda g, h, off=off: (g, (P_HEAD0 + off) // HEAD + h * (HEAD_COLS // HEAD)))

    def head_const(arr_rows):
        return pl.BlockSpec((arr_rows, HEAD), lambda g, h: (0, h))

    st_spec = pl.BlockSpec((SAMPLE_GROUP, 1, HEAD, HEAD), lambda g, h: (g, h, 0, 0))
    hconst = pl.BlockSpec((1, HEAD, HEAD), lambda g, h: (h, 0, 0))
    in_specs = [
        hcol(O_RQ), hcol(O_RK), hcol(O_RV), hcol(O_RG),
        col(P_MQ // HEAD), col(P_MK // HEAD), hcol(O_MV), hcol(O_MO),
        pl.BlockSpec((blk, HEAD), lambda g, h: (g, P_GATE // HEAD)),
        col(0), col(N_HEADS),
        st_spec, st_spec,
        pl.BlockSpec((blk, HEAD), lambda g, h: (g, h)),
        pl.BlockSpec((1, blk, 1), lambda g, h: (h, g, 0)),
        pl.BlockSpec((1, HEAD), lambda g, h: (0, 0)),
        head_const(M_CONV), pl.BlockSpec((M_CONV, HEAD), lambda g, h: (0, N_HEADS + h)),
        head_const(1), pl.BlockSpec((1, HEAD), lambda g, h: (0, N_HEADS + h)),
        head_const(1), head_const(1),
        pl.BlockSpec((blk, HEAD), lambda g, h: (0, 0)),
        pl.BlockSpec((blk, HEAD), lambda g, h: (0, 0)),
        hconst, hconst, hconst, hconst,
    ]
    out_specs = [
        pl.BlockSpec((blk, HEAD), lambda g, h: (g, h)),
        pl.BlockSpec((blk, HEAD), lambda g, h: (g, h)),
        st_spec, st_spec,
        pl.BlockSpec((blk, HEAD), lambda g, h: (g, h)),
        pl.BlockSpec((blk, HEAD), lambda g, h: (g, h)),
    ]
    out_shape = [
        jax.ShapeDtypeStruct((rows_all, GROUP_W), BF16),
        jax.ShapeDtypeStruct((rows_all, GROUP_W), BF16),
        jax.ShapeDtypeStruct(s0.shape, F32),
        jax.ShapeDtypeStruct(c0.shape, F32),
        jax.ShapeDtypeStruct((rows_all, GROUP_W), F32),
        jax.ShapeDtypeStruct((rows_all, GROUP_W), F32),
    ]
    return pl.pallas_call(
        functools.partial(_s_mix_kernel, seq=seq),
        grid=(ng, N_HEADS),
        in_specs=in_specs, out_specs=out_specs, out_shape=out_shape,
        scratch_shapes=[pltpu.VMEM((blk, HEAD), F32)],
        compiler_params=pltpu.CompilerParams(dimension_semantics=("arbitrary", "arbitrary")),
        name="sample_mix",
    )(proj, proj, proj, proj, proj, proj, proj, proj, proj, hist, hist, s0, c0, n_rows, m_rows, gbias,
      mcw, mcw, mcb, mcb, rnw, mnw, cos_t, sin_t, inner, qdec, kdec, cdec)


def _s_out_kernel(x_ref, yr_ref, ym_ref, g1_ref, sh2_ref, sc2_ref, g2_ref, n2_ref, nf_ref,
                  wout_ref, wup_ref, fcw_ref, fcb_ref, wdn_ref, fh_ref,
                  y_ref, u_ref, *, seq):
    rows = x_ref.shape[0]
    row_t = lax.broadcasted_iota(jnp.int32, (rows, D_FF), 0) & (seq - 1)
    mix = _dot(yr_ref[...], wout_ref[0:GROUP_W, :]) + _dot(ym_ref[...], wout_ref[GROUP_W:2 * GROUP_W, :])
    x1 = x_ref[...] + _repeat_rows(g1_ref[...], seq) * mix

    def conv_fn(u):
        hist = fh_ref[...]
        uc = fcb_ref[...] + fcw_ref[FF_CONV - 1:FF_CONV, :] * u
        for d in range(1, FF_CONV):
            prev = jnp.where(row_t >= d, _roll_rows(u, d), _roll_rows(hist, d - (FF_CONV - 1)))
            uc = uc + fcw_ref[FF_CONV - 1 - d:FF_CONV - d, :] * prev
        return uc

    y, u = _ffn(x1, _repeat_rows(n2_ref[...] * (1.0 + sc2_ref[...]), seq), _repeat_rows(sh2_ref[...], seq),
                _repeat_rows(g2_ref[...], seq), nf_ref[...], wup_ref, wdn_ref, conv_fn)
    y_ref[...] = y
    u_ref[...] = u


def _s_out_call(xs, yr, ym, mod_s, n2, nf, wout, wup, fcw, fcb, wdn, fhist, seq):
    rows_all = xs.shape[0]
    blk = S_OUT_ROWS
    nblk = rows_all // blk

    def modcol(cb):
        return pl.BlockSpec((blk // seq, D_MODEL), lambda i, cb=cb: (i, cb))

    in_specs = [
        pl.BlockSpec((blk, D_MODEL), lambda i: (i, 0)),
        pl.BlockSpec((blk, GROUP_W), lambda i: (i, 0)),
        pl.BlockSpec((blk, GROUP_W), lambda i: (i, 0)),
        modcol(2), modcol(3), modcol(4), modcol(5),
        _const_spec((1, D_MODEL)), _const_spec((1, D_MODEL)),
        _const_spec((2 * GROUP_W, D_MODEL)),
        _const_spec((D_MODEL, 2 * D_FF)),
        _const_spec((FF_CONV, D_FF)), _const_spec((1, D_FF)),
        _const_spec((D_FF, D_MODEL)),
        pl.BlockSpec((blk, D_FF), lambda i: (i, 0)),
    ]
    return pl.pallas_call(
        functools.partial(_s_out_kernel, seq=seq),
        grid=(nblk,),
        in_specs=in_specs,
        out_specs=[pl.BlockSpec((blk, D_MODEL), lambda i: (i, 0)),
                   pl.BlockSpec((blk, D_FF), lambda i: (i, 0))],
        out_shape=[jax.ShapeDtypeStruct((rows_all, D_MODEL), F32),
                   jax.ShapeDtypeStruct((rows_all, D_FF), F32)],
        compiler_params=pltpu.CompilerParams(
            dimension_semantics=("arbitrary",),
            vmem_limit_bytes=V7X_VMEM_BYTES - 4 * 1024 * 1024),
        name="sample_out",
    )(xs, yr, ym, mod_s, mod_s, mod_s, mod_s, n2, nf, wout, wup, fcw, fcb, wdn, fhist)


def _rope_tables(pos):
    half = HEAD // 2
    inv = ROPE_THETA ** (-np.arange(half, dtype=np.float64) / half)
    ang = np.asarray(pos, np.float64)[:, None] * inv[None, :]
    cos, sin = np.cos(ang), np.sin(ang)
    return (jnp.asarray(np.concatenate([cos, cos], axis=-1), F32),
            jnp.asarray(np.concatenate([-sin, sin], axis=-1), F32))


def _retention_consts(cc, rows):
    lg = np.log1p(-np.exp2(-5.0 - np.arange(N_HEADS, dtype=np.float64)))
    ridx = np.arange(rows)
    t = (ridx % cc).astype(np.float64)
    rel = t[:, None] - t[None, :]
    same = (ridx[:, None] // cc) == (ridx[None, :] // cc)
    inner = np.where(same & (rel >= 0), np.exp(np.maximum(rel, 0.0) * lg[:, None, None]), 0.0)
    qdec = np.exp((t + 1.0) * lg[:, None])
    kdec = np.exp((cc - 1.0 - t) * lg[:, None])
    cdec = np.exp(cc * lg)
    wide = (N_HEADS, rows, HEAD)
    return tuple(jnp.asarray(a, F32) for a in (
        inner,
        np.broadcast_to(qdec[:, :, None], wide),
        np.broadcast_to(kdec[:, :, None], wide),
        np.broadcast_to(cdec[:, None, None], wide)))


def _regroup_kernel(w3_ref, o_ref):
    gw = GROUP_W
    w_ref = w3_ref.at[0]
    o_ref[:, P_MQ:P_GATE] = _bf(w_ref[:, 4 * gw:6 * gw])
    o_ref[:, P_GATE:P_HEAD0] = jnp.zeros((o_ref.shape[0], HEAD), BF16)
    o_ref[:, P_GATE:P_GATE + 2 * N_HEADS] = _bf(w_ref[:, 8 * gw:8 * gw + 2 * N_HEADS])
    for hd in range(N_HEADS):
        for n, grp in enumerate((0, 1, 2, 3, 6, 7)):
            dst = P_HEAD0 + hd * HEAD_COLS + n * HEAD
            src = grp * gw + hd * HEAD
            o_ref[:, dst:dst + HEAD] = _bf(w_ref[:, src:src + HEAD])


def _regroup_w_in(w):
    _, d, cols = w.shape
    rows = 256
    return pl.pallas_call(
        _regroup_kernel,
        grid=(d // rows,),
        in_specs=[pl.BlockSpec((1, rows, cols), lambda r: (0, r, 0))],
        out_specs=pl.BlockSpec((rows, IN_PAD), lambda r: (r, 0)),
        out_shape=jax.ShapeDtypeStruct((d, IN_PAD), BF16),
        name="regroup_w_in",
    )(w)


def kernel(x_prompt, x_sample, c_prompt, c_sample, state_ret, state_mlstm_C, state_mlstm_n, state_mlstm_m, state_mconv, state_ffconv, w_ada, b_ada, norm1_w, norm2_w, w_in, b_igate, b_fgate, mconv_w, mconv_b, ret_norm_w, mlstm_norm_w, w_out, w_up, ffconv_w, ffconv_b, w_down, final_w):
    assert w_ada.shape[0] == 1, "single-layer kernel"
    nb, seq_p, _ = x_prompt.shape
    ns, seq_s, _ = x_sample.shape
    assert seq_p % PROMPT_TILE == 0 and seq_s == 4 and ns % SAMPLE_GROUP == 0
    rows_s = ns * seq_s

    win = _regroup_w_in(w_in)
    wout = _bf(w_out[0])
    wup = _bf(w_up[0])
    wdn = _bf(w_down[0])
    gbias = jnp.pad(jnp.concatenate([b_igate[0], b_fgate[0]]), (0, HEAD - 2 * N_HEADS))[None, :]
    n1, n2, nf = norm1_w, norm2_w, final_w[None, :]
    mcw, mcb = mconv_w[0], mconv_b
    rnw, mnw = ret_norm_w, mlstm_norm_w
    fcw, fcb = ffconv_w[0], ffconv_b

    mod = _ada_call(jnp.concatenate([c_prompt, c_sample], axis=0), w_ada[0], b_ada)
    mod_p = mod[:nb].reshape(nb, 6, D_MODEL)
    mod_s = mod[nb:]

    cos_p, sin_p = _rope_tables(np.arange(seq_p))
    y_p, ret_p, mc_p, mn_p, mm_p, mconv_t, ffconv_t = _prompt_call(
        x_prompt, mod_p, n1, n2, nf, win, gbias, mcw, mcb, rnw, mnw, wout, wup, fcw, fcb, wdn,
        cos_p, sin_p, _retention_consts(CHUNK, CHUNK))
    mconv_p = mconv_t[:, HIST - (M_CONV - 1):, :]
    ffconv_p = ffconv_t[:, HIST - (FF_CONV - 1):, :]

    xs = x_sample.reshape(rows_s, D_MODEL)
    proj_s = _s_in_call(xs, mod_s, n1, win, seq_s)
    blk = SAMPLE_GROUP * seq_s
    cos_s, sin_s = _rope_tables(PAST_LEN + (np.arange(blk) % seq_s))
    hist_m = jnp.pad(state_mconv[0], ((0, 0), (0, seq_s - (M_CONV - 1)), (0, 0))).reshape(rows_s, 2 * GROUP_W)
    n_rows = jnp.repeat(state_mlstm_n[0].reshape(ns, GROUP_W), seq_s, axis=0)
    m_rows = jnp.repeat(state_mlstm_m[0], seq_s, axis=0).T[:, :, None]
    yr, ym, ret_s, mc_s, n_new, m_new = _s_mix_call(
        proj_s, hist_m, state_ret[0], state_mlstm_C[0], n_rows, m_rows, gbias, mcw, mcb, rnw, mnw,
        cos_s, sin_s, _retention_consts(seq_s, blk), seq_s)
    hist_f = jnp.pad(state_ffconv[0], ((0, 0), (0, seq_s - (FF_CONV - 1)), (0, 0))).reshape(rows_s, D_FF)
    y_s, u_s = _s_out_call(xs, yr, ym, mod_s, n2, nf, wout, wup, fcw, fcb, wdn, hist_f, seq_s)

    mn_s = n_new[::seq_s].reshape(ns, N_HEADS, HEAD)
    mm_s = m_new[::seq_s].reshape(ns, N_HEADS, HEAD)[:, :, 0]
    mconv_s = jnp.stack([proj_s[t::seq_s, P_MQ:P_GATE] for t in range(seq_s - (M_CONV - 1), seq_s)], axis=1)
    ffconv_s = jnp.stack([u_s[t::seq_s, :] for t in range(seq_s - (FF_CONV - 1), seq_s)], axis=1)

    return (y_p, y_s.reshape(ns, seq_s, D_MODEL),
            ret_p[None], mc_p[None], mn_p[None], mm_p[:, :, 0][None], mconv_p[None], ffconv_p[None],
            ret_s[None], mc_s[None], mn_s[None], mm_s[None], mconv_s[None], ffconv_s[None])
```
